```python
import jax, jax.numpy as jnp
from jax import lax
import numpy as np

D_MODEL = 1024
BATCH = 4
SEQ = 8192
DEPTH = 1
DEC_BATCH = 8
DEC_SEQ = 8192
PAST_LEN = 128

N_META = 16
GRID_W = 64
Q_BLOCK = 128
ROPE_BASE = 10000.0
NORM_EPS = 1e-6
GQA_HEADS = D_MODEL // 128
GQA_KV_HEADS = GQA_HEADS // 4
GQA_HEAD_DIM = 64
MLA_HEADS = D_MODEL // 128
MLA_NOPE_DIM = 64
MLA_ROPE_DIM = 32
MLA_V_DIM = 64
MLA_Q_RANK = 3 * D_MODEL // 8
MLA_KV_RANK = D_MODEL // 4
N_EXPERTS = 16
CAPACITY_FACTOR = 2
EXPERT_FF = D_MODEL // 2

IN_SPLIT_SIZES = (
    GQA_HEADS * GQA_HEAD_DIM,
    GQA_KV_HEADS * GQA_HEAD_DIM,
    GQA_KV_HEADS * GQA_HEAD_DIM,
    MLA_Q_RANK,
    MLA_KV_RANK,
    MLA_ROPE_DIM,
    2 * D_MODEL,
)
IN_COLS = int(sum(IN_SPLIT_SIZES))
IN_SPLIT_IDX = tuple(int(i) for i in np.cumsum(IN_SPLIT_SIZES)[:-1])

kernel_name = 'hybrid_gqa_mla_ec_encoder'


def _rmsnorm(x, g):
    xf = x.astype(jnp.float32)
    y = xf * lax.rsqrt(jnp.mean(xf * xf, axis=-1, keepdims=True) + NORM_EPS)
    return (y * g.astype(jnp.float32)).astype(x.dtype)


def _rope_1d(x, pos):
    half = x.shape[-1] // 2
    inv = ROPE_BASE ** (-jnp.arange(half, dtype=jnp.float32) / half)
    ang = pos[:, None] * inv[None, :]
    cos = jnp.cos(ang)[None, :, None, :]
    sin = jnp.sin(ang)[None, :, None, :]
    xf = x.astype(jnp.float32)
    x1, x2 = xf[..., :half], xf[..., half:]
    return jnp.concatenate([x1 * cos - x2 * sin, x2 * cos + x1 * sin], axis=-1).astype(x.dtype)


def _rope_2d(x, pos_row, pos_col):
    a = x.shape[-1] // 2
    return jnp.concatenate([_rope_1d(x[..., :a], pos_row), _rope_1d(x[..., a:], pos_col)], axis=-1)


def _sweep_query_blocks(q, attend):
    B, L = q.shape[0], q.shape[1]
    nb = -(-L // Q_BLOCK)
    pad = nb * Q_BLOCK - L
    qp = jnp.pad(q, [(0, 0), (0, pad)] + [(0, 0)] * (q.ndim - 2))
    qb = jnp.moveaxis(qp.reshape((B, nb, Q_BLOCK) + q.shape[2:]), 1, 0)
    out = lax.map(attend, qb)
    out = jnp.moveaxis(out, 0, 1).reshape(B, nb * Q_BLOCK, -1)
    return out[:, :L]


def _expert_choice_ffn(h, w_router, w_gate, w_up, w_down):
    B, L, D = h.shape
    n = B * L
    cap = (CAPACITY_FACTOR * n) // N_EXPERTS
    t = h.reshape(n, D)
    aff = jax.nn.softmax(jnp.matmul(t, w_router).astype(jnp.float32), axis=-1)
    g, idx = lax.top_k(aff.T, cap)
    xs = t[idx]
    hg = jnp.einsum('ecd,edf->ecf', xs, w_gate)
    hu = jnp.einsum('ecd,edf->ecf', xs, w_up)
    y = jnp.einsum('ecf,efd->ecd', jax.nn.silu(hg) * hu, w_down)
    y = y * g[..., None].astype(y.dtype)
    out = jnp.zeros((n, D), y.dtype).at[idx.reshape(-1)].add(y.reshape(-1, D))
    return out.reshape(B, L, D)


def _layer(h, pos_row, pos_col, attn_norm, w_in, gqa_q_norm, gqa_k_norm, mla_q_norm,
           mla_kv_norm, mla_w_uq, mla_w_ukv, w_o_gqa, w_o_mla, w_out, ffn_norm,
           w_router, w_gate, w_up, w_down):
    B, L, D = h.shape
    hn = _rmsnorm(h, attn_norm)
    proj = jnp.matmul(hn, w_in)
    q_g, k_g, v_g, c_q, c_kv, k_r, gates = jnp.split(proj, IN_SPLIT_IDX, axis=-1)

    q_g = _rope_2d(_rmsnorm(q_g.reshape(B, L, GQA_HEADS, GQA_HEAD_DIM), gqa_q_norm), pos_row, pos_col)
    k_g = _rope_2d(_rmsnorm(k_g.reshape(B, L, GQA_KV_HEADS, GQA_HEAD_DIM), gqa_k_norm), pos_row, pos_col)
    v_g = v_g.reshape(B, L, GQA_KV_HEADS, GQA_HEAD_DIM)
    rep = GQA_HEADS // GQA_KV_HEADS
    scale_g = 1.0 / float(np.sqrt(GQA_HEAD_DIM))

    def attend_gqa(qb):
        qb = qb.reshape(B, Q_BLOCK, GQA_KV_HEADS, rep, GQA_HEAD_DIM)
        s = jnp.einsum('bqgrd,bkgd->bgrqk', qb, k_g, preferred_element_type=jnp.float32) * scale_g
        p = jax.nn.softmax(s, axis=-1).astype(v_g.dtype)
        o = jnp.einsum('bgrqk,bkgd->bqgrd', p, v_g)
        return o.reshape(B, Q_BLOCK, GQA_HEADS * GQA_HEAD_DIM)

    o_g = _sweep_query_blocks(q_g, attend_gqa)

    q_m = jnp.matmul(_rmsnorm(c_q, mla_q_norm), mla_w_uq).reshape(B, L, MLA_HEADS, MLA_NOPE_DIM + MLA_ROPE_DIM)
    q_m = jnp.concatenate([q_m[..., :MLA_NOPE_DIM], _rope_2d(q_m[..., MLA_NOPE_DIM:], pos_row, pos_col)], axis=-1)
    kv = jnp.matmul(_rmsnorm(c_kv, mla_kv_norm), mla_w_ukv).reshape(B, L, MLA_HEADS, MLA_NOPE_DIM + MLA_V_DIM)
    k_nope, v_m = kv[..., :MLA_NOPE_DIM], kv[..., MLA_NOPE_DIM:]
    k_rope = _rope_2d(k_r[:, :, None, :], pos_row, pos_col)[:, :, 0, :]
    scale_m = 1.0 / float(np.sqrt(MLA_NOPE_DIM + MLA_ROPE_DIM))

    def attend_mla(qb):
        qn, qr = qb[..., :MLA_NOPE_DIM], qb[..., MLA_NOPE_DIM:]
        s = (jnp.einsum('bqhd,bkhd->bhqk', qn, k_nope, preferred_element_type=jnp.float32)
             + jnp.einsum('bqhd,bkd->bhqk', qr, k_rope, preferred_element_type=jnp.float32)) * scale_m
        p = jax.nn.softmax(s, axis=-1).astype(v_m.dtype)
        o = jnp.einsum('bhqk,bkhd->bqhd', p, v_m)
        return o.reshape(B, Q_BLOCK, MLA_HEADS * MLA_V_DIM)

    o_m = _sweep_query_blocks(q_m, attend_mla)

    g_a, g_b = jnp.split(gates, 2, axis=-1)
    mixed = (jax.nn.sigmoid(g_a) * jnp.matmul(o_g, w_o_gqa)
             + jax.nn.sigmoid(g_b) * jnp.matmul(o_m, w_o_mla))
    h = h + jnp.matmul(mixed, w_out)

    h = h + _expert_choice_ffn(_rmsnorm(h, ffn_norm), w_router, w_gate, w_up, w_down)
    return h


def _forward(x, meta_tokens, attn_norm, w_in, gqa_q_norm, gqa_k_norm, mla_q_norm, mla_kv_norm,
             mla_w_uq, mla_w_ukv, w_o_gqa, w_o_mla, w_out, ffn_norm, w_router, w_gate, w_up,
             w_down, final_norm):
    B, S, D = x.shape
    rows_n = S // GRID_W
    grid_row = jnp.repeat(jnp.arange(rows_n, dtype=jnp.float32), GRID_W)
    grid_col = jnp.tile(jnp.arange(GRID_W, dtype=jnp.float32), rows_n)
    pos_row = jnp.concatenate([jnp.full((N_META,), -1.0, jnp.float32), grid_row])
    pos_col = jnp.concatenate([jnp.arange(N_META, dtype=jnp.float32), grid_col])
    meta = jnp.broadcast_to(meta_tokens.astype(x.dtype)[None], (B, N_META, D))
    h = jnp.concatenate([meta, x], axis=1)
    for l in range(DEPTH):
        h = _layer(h, pos_row, pos_col, attn_norm[l], w_in[l], gqa_q_norm[l], gqa_k_norm[l],
                   mla_q_norm[l], mla_kv_norm[l], mla_w_uq[l], mla_w_ukv[l], w_o_gqa[l],
                   w_o_mla[l], w_out[l], ffn_norm[l], w_router[l], w_gate[l], w_up[l], w_down[l])
    h = _rmsnorm(h, final_norm)
    return h[:, N_META:]


def setup_inputs(seed: int = 0) -> dict:
    key = jax.random.key(seed)
    ks = jax.random.split(key, 24)
    f32 = jnp.float32

    def nrm(k, shape, scale):
        return jax.random.normal(k, shape, f32) * scale

    def gain(k, shape):
        return 1.0 + 0.05 * jax.random.normal(k, shape, f32)

    Dp = DEPTH
    return {
        'x_prompt': nrm(ks[0], (BATCH, SEQ, D_MODEL), 1.0),
        'x_sample': nrm(ks[1], (DEC_BATCH, DEC_SEQ, D_MODEL), 1.0),
        'meta_tokens': nrm(ks[2], (N_META, D_MODEL), 1.0),
        'attn_norm': gain(ks[3], (Dp, D_MODEL)),
        'w_in': nrm(ks[4], (Dp, D_MODEL, IN_COLS), D_MODEL ** -0.5),
        'gqa_q_norm': gain(ks[5], (Dp, GQA_HEAD_DIM)),
        'gqa_k_norm': gain(ks[6], (Dp, GQA_HEAD_DIM)),
        'mla_q_norm': gain(ks[7], (Dp, MLA_Q_RANK)),
        'mla_kv_norm': gain(ks[8], (Dp, MLA_KV_RANK)),
        'mla_w_uq': nrm(ks[9], (Dp, MLA_Q_RANK, MLA_HEADS * (MLA_NOPE_DIM + MLA_ROPE_DIM)), MLA_Q_RANK ** -0.5),
        'mla_w_ukv': nrm(ks[10], (Dp, MLA_KV_RANK, MLA_HEADS * (MLA_NOPE_DIM + MLA_V_DIM)), MLA_KV_RANK ** -0.5),
        'w_o_gqa': nrm(ks[11], (Dp, GQA_HEADS * GQA_HEAD_DIM, D_MODEL), (GQA_HEADS * GQA_HEAD_DIM) ** -0.5),
        'w_o_mla': nrm(ks[12], (Dp, MLA_HEADS * MLA_V_DIM, D_MODEL), (MLA_HEADS * MLA_V_DIM) ** -0.5),
        'w_out': nrm(ks[13], (Dp, D_MODEL, D_MODEL), D_MODEL ** -0.5),
        'ffn_norm': gain(ks[14], (Dp, D_MODEL)),
        'w_router': nrm(ks[15], (Dp, D_MODEL, N_EXPERTS), D_MODEL ** -0.5),
        'w_gate': nrm(ks[16], (Dp, N_EXPERTS, D_MODEL, EXPERT_FF), D_MODEL ** -0.5),
        'w_up': nrm(ks[17], (Dp, N_EXPERTS, D_MODEL, EXPERT_FF), D_MODEL ** -0.5),
        'w_down': nrm(ks[18], (Dp, N_EXPERTS, EXPERT_FF, D_MODEL), EXPERT_FF ** -0.5),
        'final_norm': gain(ks[19], (D_MODEL,)),
    }


def reference(x_prompt, x_sample, meta_tokens, attn_norm, w_in, gqa_q_norm, gqa_k_norm,
              mla_q_norm, mla_kv_norm, mla_w_uq, mla_w_ukv, w_o_gqa, w_o_mla, w_out,
              ffn_norm, w_router, w_gate, w_up, w_down, final_norm):
    y_prompt = _forward(x_prompt, meta_tokens, attn_norm, w_in, gqa_q_norm, gqa_k_norm,
                        mla_q_norm, mla_kv_norm, mla_w_uq, mla_w_ukv, w_o_gqa, w_o_mla,
                        w_out, ffn_norm, w_router, w_gate, w_up, w_down, final_norm)
    y_sample = _forward(x_sample, meta_tokens, attn_norm, w_in, gqa_q_norm, gqa_k_norm,
                        mla_q_norm, mla_kv_norm, mla_w_uq, mla_w_ukv, w_o_gqa, w_o_mla,
                        w_out, ffn_norm, w_router, w_gate, w_up, w_down, final_norm)
    return (y_prompt, y_sample)
```

```python
import functools

import jax
import jax.numpy as jnp
import numpy as np
from jax import lax
from jax.experimental import pallas as pl
from jax.experimental.pallas import tpu as pltpu

F32 = jnp.float32
BF16 = jnp.bfloat16

D_MODEL = 1024
N_META = 16
GRID_W = 64
ROPE_BASE = 10000.0
NORM_EPS = 1e-6
GQA_HEADS = 8
GQA_KV_HEADS = 2
HEAD_DIM = 64
MLA_HEADS = 8
MLA_NOPE = 64
MLA_ROPE = 32
MLA_V = 64
MLA_Q_RANK = 384
MLA_KV_RANK = 256
N_EXPERTS = 16
CAPACITY_FACTOR = 2
EXPERT_FF = 512
LANES = 128
N_HEAD_SLOTS = GQA_HEADS + MLA_HEADS
N_K_SLABS = 1 + MLA_HEADS
N_V_SLABS = 2 + MLA_HEADS // 2
VMEM_LIMIT = 56 * 1024 * 1024
NEG_BIG = -1e30


def _cparams(sem):
    return pltpu.CompilerParams(dimension_semantics=sem, vmem_limit_bytes=VMEM_LIMIT)


def _rms(x, gain):
    return x * lax.rsqrt(jnp.mean(x * x, axis=-1, keepdims=True) + NORM_EPS) * gain


def _rope(x, cos, sin_signed, half):
    lane = lax.broadcasted_iota(jnp.int32, x.shape, 1)
    first = (lane % (2 * half)) < half
    partner = jnp.where(first, pltpu.roll(x, LANES - half, 1), pltpu.roll(x, half, 1))
    return x * cos + partner * sin_signed


def _split_dot(x, w_bf16):
    hi = x.astype(BF16)
    lo = (x - hi.astype(F32)).astype(BF16)
    return (jnp.dot(hi, w_bf16, preferred_element_type=F32)
            + jnp.dot(lo, w_bf16, preferred_element_type=F32))


def _proj_kernel(h_ref, an_ref, w1_ref, wg_ref, wuq_ref, wuk_ref, wuv_ref,
                 gq_ref, gk_ref, nq_ref, nkv_ref, cg_ref, sg_ref, cm_ref, sm_ref,
                 q_ref, kt_ref, v_ref, g_ref):
    hn = _rms(h_ref[0], an_ref[...]).astype(BF16)
    p1 = jnp.dot(hn, w1_ref[...], preferred_element_type=F32)
    g_ref[0] = jax.nn.sigmoid(jnp.dot(hn, wg_ref[...], preferred_element_type=F32)).astype(BF16)

    r = lax.broadcasted_iota(jnp.int32, (LANES, LANES), 0) // HEAD_DIM
    c = lax.broadcasted_iota(jnp.int32, (LANES, LANES), 1) // HEAD_DIM
    same_head = (r == c).astype(BF16)
    cg, sg, cm, sm = cg_ref[...], sg_ref[...], cm_ref[...], sm_ref[...]

    def head_norm_rope(x, gain):
        ms = _split_dot(x * x, same_head) * (1.0 / HEAD_DIM)
        return _rope(x * lax.rsqrt(ms + NORM_EPS) * gain, cg, sg, HEAD_DIM // 4)

    scale_g = 1.0 / float(np.sqrt(HEAD_DIM))
    for h in range(GQA_HEADS):
        x = p1[:, h * LANES:(h + 1) * LANES]
        q_ref[0, h] = (head_norm_rope(x, gq_ref[...]) * scale_g).astype(BF16)
    o = GQA_HEADS * LANES
    kt_ref[0, 0, 0] = head_norm_rope(p1[:, o:o + LANES], gk_ref[...]).T.astype(BF16)
    v_ref[0, 0] = p1[:, o + LANES:o + 2 * LANES].astype(BF16)
    v_ref[0, 1] = p1[:, o + 2 * LANES:o + 3 * LANES].astype(BF16)
    k_rope = _rope(p1[:, o + 3 * LANES:o + 4 * LANES], cm, sm, MLA_ROPE // 4)
    o += 4 * LANES
    cq = _rms(p1[:, o:o + MLA_Q_RANK], nq_ref[...]).astype(BF16)
    ckv = _rms(p1[:, o + MLA_Q_RANK:o + MLA_Q_RANK + MLA_KV_RANK], nkv_ref[...]).astype(BF16)

    scale_m = 1.0 / float(np.sqrt(MLA_NOPE + MLA_ROPE))
    qm = jnp.dot(cq, wuq_ref[...], preferred_element_type=F32)
    for h in range(MLA_HEADS):
        x = qm[:, h * LANES:(h + 1) * LANES]
        q_ref[0, GQA_HEADS + h] = (_rope(x, cm, sm, MLA_ROPE // 4) * scale_m).astype(BF16)
    km = jnp.dot(ckv, wuk_ref[...], preferred_element_type=F32)
    for h in range(MLA_HEADS):
        kt_ref[0, 1 + h, 0] = (km[:, h * LANES:(h + 1) * LANES] + k_rope).T.astype(BF16)
    vm = jnp.dot(ckv, wuv_ref[...], preferred_element_type=F32)
    for j in range(MLA_HEADS // 2):
        v_ref[0, 2 + j] = vm[:, j * LANES:(j + 1) * LANES].astype(BF16)


def _projection(h, an, w1, wg, wuq, wuk, wuv, gq, gk, nq, nkv, cg, sg, cm, sm, tm):
    nb, lp, d = h.shape
    nt = lp // tm
    full = lambda a: pl.BlockSpec(a.shape, lambda b, i: (0,) * a.ndim)
    tab = pl.BlockSpec((tm, LANES), lambda b, i: (i, 0))
    return pl.pallas_call(
        _proj_kernel,
        grid=(nb, nt),
        in_specs=[pl.BlockSpec((1, tm, d), lambda b, i: (b, i, 0)),
                  full(an), full(w1), full(wg), full(wuq), full(wuk), full(wuv),
                  full(gq), full(gk), full(nq), full(nkv), tab, tab, tab, tab],
        out_specs=[pl.BlockSpec((1, N_HEAD_SLOTS, tm, LANES), lambda b, i: (b, 0, i, 0)),
                   pl.BlockSpec((1, N_K_SLABS, 1, LANES, tm), lambda b, i: (b, 0, i, 0, 0)),
                   pl.BlockSpec((1, N_V_SLABS, tm, LANES), lambda b, i: (b, 0, i, 0)),
                   pl.BlockSpec((1, tm, 2 * d), lambda b, i: (b, i, 0))],
        out_shape=[jax.ShapeDtypeStruct((nb, N_HEAD_SLOTS, lp, LANES), BF16),
                   jax.ShapeDtypeStruct((nb, N_K_SLABS, nt, LANES, tm), BF16),
                   jax.ShapeDtypeStruct((nb, N_V_SLABS, lp, LANES), BF16),
                   jax.ShapeDtypeStruct((nb, lp, 2 * d), BF16)],
        compiler_params=_cparams(("parallel", "parallel")),
        name="projection",
    )(h, an, w1, wg, wuq, wuk, wuv, gq, gk, nq, nkv, cg, sg, cm, sm)


def _attn_kernel(q_ref, kta_ref, ktb_ref, va_ref, vb_ref, o_ref, *, n_blocks, tk, valid_last):
    tq = q_ref.shape[2]

    def one_head(q, kt_ref, v_ref):
        def step(j, carry, masked):
            m, l, acc = carry
            s = jnp.dot(q, kt_ref[0, 0, j], preferred_element_type=F32)
            if masked:
                col = lax.broadcasted_iota(jnp.int32, s.shape, 1)
                s = jnp.where(col < valid_last, s, NEG_BIG)
            m_new = jnp.maximum(m, jnp.max(s, axis=1, keepdims=True))
            alpha = jnp.exp(m - m_new)
            p = jnp.exp(s - m_new)
            l = alpha * l + jnp.sum(p, axis=1, keepdims=True)
            v = v_ref[0, 0, pl.ds(pl.multiple_of(j * tk, tk), tk), :]
            acc = alpha * acc + jnp.dot(p.astype(BF16), v, preferred_element_type=F32)
            return m_new, l, acc

        init = (jnp.full((tq, 1), NEG_BIG, F32), jnp.zeros((tq, 1), F32),
                jnp.zeros((tq, LANES), F32))
        carry = lax.fori_loop(0, n_blocks - 1, functools.partial(step, masked=False), init)
        m, l, acc = step(n_blocks - 1, carry, masked=True)
        return acc / l

    o_a = one_head(q_ref[0, 0], kta_ref, va_ref)
    o_b = one_head(q_ref[0, 1], ktb_ref, vb_ref)
    lane = lax.broadcasted_iota(jnp.int32, o_a.shape, 1)
    o_ref[0] = jnp.where(lane < HEAD_DIM, o_a, o_b).astype(BF16)


def _k_slab(h):
    return jnp.where(h < GQA_HEADS, 0, h - (GQA_HEADS - 1))


def _v_slab(h):
    group = h // (GQA_HEADS // GQA_KV_HEADS)
    gqa = jnp.where(group == h % 2, 0, 1)
    return jnp.where(h < GQA_HEADS, gqa, 2 + (h - GQA_HEADS) // 2)


def _attention(q, kt, v, seq_len, tq):
    nb, _, lp, _ = q.shape
    nt, tk = kt.shape[2], kt.shape[4]
    valid_last = seq_len - (nt - 1) * tk
    assert 0 < valid_last <= tk
    kspec = lambda par: pl.BlockSpec((1, 1, nt, LANES, tk),
                                     lambda b, j, i: (b, _k_slab(2 * j + par), 0, 0, 0))
    vspec = lambda par: pl.BlockSpec((1, 1, lp, LANES),
                                     lambda b, j, i: (b, _v_slab(2 * j + par), 0, 0))
    return pl.pallas_call(
        functools.partial(_attn_kernel, n_blocks=nt, tk=tk, valid_last=valid_last),
        grid=(nb, N_HEAD_SLOTS // 2, lp // tq),
        in_specs=[pl.BlockSpec((1, 2, tq, LANES), lambda b, j, i: (b, j, i, 0)),
                  kspec(0), kspec(1), vspec(0), vspec(1)],
        out_specs=pl.BlockSpec((1, tq, LANES), lambda b, j, i: (b, i, j)),
        out_shape=jax.ShapeDtypeStruct((nb, lp, (N_HEAD_SLOTS // 2) * LANES), BF16),
        compiler_params=_cparams(("parallel", "parallel", "parallel")),
        name="attention",
    )(q, kt, kt, v, v)


def _post_kernel(o_ref, g_ref, h_ref, wog_ref, wom_ref, wout_ref, fn_ref, wrh_ref, wrl_ref,
                 h2_ref, xn_ref, aff_ref, *, seq_len):
    tm = o_ref.shape[1]
    half = o_ref.shape[2] // 2
    o = o_ref[0]
    a = jnp.dot(o[:, :half], wog_ref[...], preferred_element_type=F32)
    b = jnp.dot(o[:, half:], wom_ref[...], preferred_element_type=F32)
    g = g_ref[0].astype(F32)
    mixed = (g[:, :D_MODEL] * a + g[:, D_MODEL:] * b).astype(BF16)
    h2 = h_ref[0] + jnp.dot(mixed, wout_ref[...], preferred_element_type=F32)
    h2_ref[0] = h2
    xn = _rms(h2, fn_ref[...])
    xn_ref[0] = xn.astype(BF16)
    hi = xn.astype(BF16)
    lo = (xn - hi.astype(F32)).astype(BF16)
    logits = (jnp.dot(hi, wrh_ref[...], preferred_element_type=F32)
              + jnp.dot(lo, wrh_ref[...], preferred_element_type=F32)
              + jnp.dot(hi, wrl_ref[...], preferred_element_type=F32))
    lane = lax.broadcasted_iota(jnp.int32, logits.shape, 1)
    logits = jnp.where(lane < N_EXPERTS, logits, NEG_BIG)
    e = jnp.exp(logits - jnp.max(logits, axis=1, keepdims=True))
    aff = e / jnp.sum(e, axis=1, keepdims=True)
    row = pl.program_id(1) * tm + lax.broadcasted_iota(jnp.int32, logits.shape, 0)
    aff = jnp.where(row < seq_len, aff, -1.0)
    aff_ref[...] = aff.T[:N_EXPERTS]


def _post(o, g, h, wog, wom, wout, fn, wrh, wrl, seq_len, tm):
    nb, lp, d = h.shape
    nt = lp // tm
    full = lambda a: pl.BlockSpec(a.shape, lambda b, i: (0,) * a.ndim)
    row = lambda w: pl.BlockSpec((1, tm, w), lambda b, i: (b, i, 0))
    return pl.pallas_call(
        functools.partial(_post_kernel, seq_len=seq_len),
        grid=(nb, nt),
        in_specs=[row(d), row(2 * d), row(d), full(wog), full(wom), full(wout), full(fn),
                  full(wrh), full(wrl)],
        out_specs=[row(d), row(d), pl.BlockSpec((N_EXPERTS, tm), lambda b, i: (0, b * nt + i))],
        out_shape=[jax.ShapeDtypeStruct((nb, lp, d), F32),
                   jax.ShapeDtypeStruct((nb, lp, d), BF16),
                   jax.ShapeDtypeStruct((N_EXPERTS, nb * lp), F32)],
        compiler_params=_cparams(("parallel", "parallel")),
        name="merge_router",
    )(o, g, h, wog, wom, wout, fn, wrh, wrl)


def _ffn_kernel(x_ref, gate_ref, wg_ref, wu_ref, wd_ref, y_ref):
    x = x_ref[0]
    hg = jnp.dot(x, wg_ref[0], preferred_element_type=F32)
    hu = jnp.dot(x, wu_ref[0], preferred_element_type=F32)
    act = (hg * jax.nn.sigmoid(hg) * hu).astype(BF16)
    y_ref[0] = jnp.dot(act, wd_ref[0], preferred_element_type=F32) * gate_ref[0]


def _expert_ffn(xs, gate, wg, wu, wd, tc):
    ne, cp, d = xs.shape
    f = wg.shape[2]
    return pl.pallas_call(
        _ffn_kernel,
        grid=(ne, cp // tc),
        in_specs=[pl.BlockSpec((1, tc, d), lambda e, c: (e, c, 0)),
                  pl.BlockSpec((1, tc, 1), lambda e, c: (e, c, 0)),
                  pl.BlockSpec((1, d, f), lambda e, c: (e, 0, 0)),
                  pl.BlockSpec((1, d, f), lambda e, c: (e, 0, 0)),
                  pl.BlockSpec((1, f, d), lambda e, c: (e, 0, 0))],
        out_specs=pl.BlockSpec((1, tc, d), lambda e, c: (e, c, 0)),
        out_shape=jax.ShapeDtypeStruct((ne, cp, d), F32),
        compiler_params=_cparams(("parallel", "parallel")),
        name="expert_ffn",
    )(xs, gate, wg, wu, wd)


def _final_kernel(h_ref, gain_ref, y_ref):
    y_ref[0] = _rms(h_ref[0], gain_ref[...])


def _final_norm(h, gain, seq, tm):
    nb, lp, d = h.shape
    return pl.pallas_call(
        _final_kernel,
        grid=(nb, seq // tm),
        in_specs=[pl.BlockSpec((1, tm, d), lambda b, i: (b, i, 0)),
                  pl.BlockSpec(gain.shape, lambda b, i: (0, 0))],
        out_specs=pl.BlockSpec((1, tm, d), lambda b, i: (b, i, 0)),
        out_shape=jax.ShapeDtypeStruct((nb, seq, d), F32),
        compiler_params=_cparams(("parallel", "parallel")),
        name="final_norm",
    )(h, gain)


def _rope_tables(seq, lp):
    j = jnp.arange(lp)
    grid_tok = j < seq
    meta_tok = (j >= seq) & (j < seq + N_META)
    row = jnp.where(grid_tok, j // GRID_W, jnp.where(meta_tok, -1, 0)).astype(F32)
    col = jnp.where(grid_tok, j % GRID_W, jnp.where(meta_tok, j - seq, 0)).astype(F32)

    def axis_tables(pos, half):
        inv = ROPE_BASE ** (-jnp.arange(half, dtype=F32) / half)
        ang = pos[:, None] * inv[None, :]
        cos, sin = jnp.cos(ang), jnp.sin(ang)
        return jnp.concatenate([cos, cos], 1), jnp.concatenate([-sin, sin], 1)

    def both_axes(half):
        cr, sr = axis_tables(row, half)
        cc, sc = axis_tables(col, half)
        return jnp.concatenate([cr, cc], 1), jnp.concatenate([sr, sc], 1)

    cg, sg = both_axes(HEAD_DIM // 4)
    cg, sg = jnp.tile(cg, (1, 2)), jnp.tile(sg, (1, 2))
    cm32, sm32 = both_axes(MLA_ROPE // 4)
    ones = jnp.ones((lp, MLA_NOPE), F32)
    zeros = jnp.zeros((lp, MLA_NOPE), F32)
    tail1 = jnp.ones((lp, LANES - MLA_NOPE - MLA_ROPE), F32)
    tail0 = jnp.zeros((lp, LANES - MLA_NOPE - MLA_ROPE), F32)
    cm = jnp.concatenate([ones, cm32, tail1], 1)
    sm = jnp.concatenate([zeros, sm32, tail0], 1)
    return cg, sg, cm, sm


def _prep_weights(w_in, w_uq, w_ukv):
    d = w_in.shape[0]
    splits = np.cumsum([GQA_HEADS * HEAD_DIM, GQA_KV_HEADS * HEAD_DIM, GQA_KV_HEADS * HEAD_DIM,
                        MLA_Q_RANK, MLA_KV_RANK, MLA_ROPE])
    wq, wk, wv, wcq, wckv, wkr, wgate = jnp.split(w_in, splits, axis=1)
    wq = wq.reshape(d, GQA_HEADS, HEAD_DIM)
    zq = jnp.zeros_like(wq)
    rep = GQA_HEADS // GQA_KV_HEADS
    in_g0 = (jnp.arange(GQA_HEADS) < rep)[None, :, None]
    wq = jnp.concatenate([jnp.where(in_g0, wq, zq), jnp.where(in_g0, zq, wq)], axis=2)
    wq = wq.reshape(d, GQA_HEADS * LANES)
    wv_swapped = jnp.concatenate([wv[:, HEAD_DIM:], wv[:, :HEAD_DIM]], axis=1)
    wkr = jnp.pad(wkr, ((0, 0), (MLA_NOPE, LANES - MLA_NOPE - MLA_ROPE)))
    w1 = jnp.concatenate([wq, wk, wv, wv_swapped, wkr, wcq, wckv], axis=1).astype(BF16)

    wuq = w_uq.reshape(MLA_Q_RANK, MLA_HEADS, MLA_NOPE + MLA_ROPE)
    wuq = jnp.pad(wuq, ((0, 0), (0, 0), (0, LANES - MLA_NOPE - MLA_ROPE)))
    wuq = wuq.reshape(MLA_Q_RANK, MLA_HEADS * LANES).astype(BF16)
    wukv = w_ukv.reshape(MLA_KV_RANK, MLA_HEADS, MLA_NOPE + MLA_V)
    wuk = jnp.pad(wukv[:, :, :MLA_NOPE], ((0, 0), (0, 0), (0, LANES - MLA_NOPE)))
    wuk = wuk.reshape(MLA_KV_RANK, MLA_HEADS * LANES).astype(BF16)
    wuv = wukv[:, :, MLA_NOPE:].reshape(MLA_KV_RANK, MLA_HEADS * MLA_V).astype(BF16)
    return w1, wgate.astype(BF16), wuq, wuk, wuv


def _token_tile(seq):
    for t in (640, 512, 384, 256, 128):
        lp = -(-(seq + N_META) // t) * t
        if lp - t < seq + N_META and seq % t % 16 == 0:
            return t, lp
    raise ValueError(seq)


def _chunk(cp):
    for t in range(2048, 0, -128):
        if cp % t == 0:
            return t
    raise ValueError(cp)


def kernel(x_prompt, x_sample, meta_tokens, attn_norm, w_in, gqa_q_norm, gqa_k_norm, mla_q_norm,
           mla_kv_norm, mla_w_uq, mla_w_ukv, w_o_gqa, w_o_mla, w_out, ffn_norm, w_router, w_gate,
           w_up, w_down, final_norm):
    seq = x_prompt.shape[1]
    assert x_sample.shape[1] == seq and attn_norm.shape[0] == 1
    d = x_prompt.shape[2]
    seq_len = seq + N_META
    tm, lp = _token_tile(seq)
    groups = (x_prompt, x_sample)
    nbs = [g.shape[0] for g in groups]
    nb = sum(nbs)

    meta = meta_tokens.astype(F32)[None]
    h = jnp.concatenate([
        jnp.concatenate([g, jnp.broadcast_to(meta, (g.shape[0], N_META, d)),
                         jnp.zeros((g.shape[0], lp - seq_len, d), F32)], axis=1)
        for g in groups], axis=0)

    w1, wgate, wuq, wuk, wuv = _prep_weights(w_in[0], mla_w_uq[0], mla_w_ukv[0])
    cg, sg, cm, sm = _rope_tables(seq, lp)
    tile2 = lambda v: jnp.tile(v[0].astype(F32), 2)[None]
    q, kt, v, gates = _projection(
        h, attn_norm[0][None], w1, wgate, wuq, wuk, wuv, tile2(gqa_q_norm), tile2(gqa_k_norm),
        mla_q_norm[0][None], mla_kv_norm[0][None], cg, sg, cm, sm, tm)
    o = _attention(q, kt, v, seq_len, tm)

    wr = jnp.pad(w_router[0], ((0, 0), (0, LANES - N_EXPERTS)))
    wrh = wr.astype(BF16)
    wrl = (wr - wrh.astype(F32)).astype(BF16)
    h2, xn, aff_t = _post(o, gates, h, w_o_gqa[0].astype(BF16), w_o_mla[0].astype(BF16),
                          w_out[0].astype(BF16), ffn_norm[0][None], wrh, wrl, seq_len, tm)

    wg_e, wu_e, wd_e = w_gate[0].astype(BF16), w_up[0].astype(BF16), w_down[0].astype(BF16)
    outs = []
    b0 = 0
    for nbg in nbs:
        n = nbg * seq_len
        cap = (CAPACITY_FACTOR * n) // N_EXPERTS
        cp = -(-cap // LANES) * LANES
        aff_g = aff_t[:, b0 * lp:(b0 + nbg) * lp]
        gate, idx = lax.top_k(aff_g, cap)
        gate = jnp.pad(gate, ((0, 0), (0, cp - cap)))
        idx = jnp.pad(idx, ((0, 0), (0, cp - cap)))
        xg = xn[b0:b0 + nbg].reshape(nbg * lp, d)
        y = _expert_ffn(xg[idx], gate[..., None], wg_e, wu_e, wd_e, _chunk(cp))
        hg = h2[b0:b0 + nbg].reshape(nbg * lp, d)
        hg = hg.at[idx.reshape(-1)].add(y.reshape(-1, d))
        outs.append(_final_norm(hg.reshape(nbg, lp, d), final_norm[None], seq, 512))
        b0 += nbg
    return tuple(outs)
```

```python
import functools

import jax
import jax.numpy as jnp
import numpy as np
from jax import lax
from jax.experimental import pallas as pl
from jax.experimental.pallas import tpu as pltpu

F32 = jnp.float32
BF16 = jnp.bfloat16

D_MODEL = 1024
N_META = 16
GRID_W = 64
ROPE_BASE = 10000.0
NORM_EPS = 1e-6
GQA_HEADS = 8
GQA_KV_HEADS = 2
HEAD_DIM = 64
MLA_HEADS = 8
MLA_NOPE = 64
MLA_ROPE = 32
MLA_V = 64
MLA_Q_RANK = 384
MLA_KV_RANK = 256
N_EXPERTS = 16
CAPACITY_FACTOR = 2
EXPERT_FF = 512
LANES = 128
N_HEAD_SLOTS = GQA_HEADS + MLA_HEADS
N_K_SLABS = 1 + MLA_HEADS
N_V_SLABS = 2 * GQA_KV_HEADS + MLA_HEADS
LOG2E = float(np.log2(np.e))
VMEM_LIMIT = 56 * 1024 * 1024
NEG_BIG = -1e30


def _cparams(sem):
    return pltpu.CompilerParams(dimension_semantics=sem, vmem_limit_bytes=VMEM_LIMIT)


def _resident(a):
    return pl.BlockSpec(a.shape, lambda *_: (0,) * a.ndim, pipeline_mode=pl.Buffered(1))


def _rms(x, gain):
    return x * lax.rsqrt(jnp.mean(x * x, axis=-1, keepdims=True) + NORM_EPS) * gain


def _rope(x, cos, sin_signed, half):
    lane = lax.broadcasted_iota(jnp.int32, x.shape, 1)
    first = (lane % (2 * half)) < half
    partner = jnp.where(first, pltpu.roll(x, LANES - half, 1), pltpu.roll(x, half, 1))
    return x * cos + partner * sin_signed


def _split_dot(x, w_bf16):
    hi = x.astype(BF16)
    lo = (x - hi.astype(F32)).astype(BF16)
    return (jnp.dot(hi, w_bf16, preferred_element_type=F32)
            + jnp.dot(lo, w_bf16, preferred_element_type=F32))


def _proj_kernel(h_ref, an_ref, w1_ref, wg_ref, wuq_ref, wuk_ref, wuv_ref,
                 gq_ref, gk_ref, nq_ref, nkv_ref, cg_ref, sg_ref, cm_ref, sm_ref,
                 q_ref, kt_ref, v_ref, g_ref):
    hn = _rms(h_ref[0], an_ref[...]).astype(BF16)
    p1 = jnp.dot(hn, w1_ref[...], preferred_element_type=F32)
    g_ref[0] = jax.nn.sigmoid(jnp.dot(hn, wg_ref[...], preferred_element_type=F32)).astype(BF16)

    r = lax.broadcasted_iota(jnp.int32, (LANES, LANES), 0) // HEAD_DIM
    c = lax.broadcasted_iota(jnp.int32, (LANES, LANES), 1) // HEAD_DIM
    same_head = (r == c).astype(BF16)
    cg, sg, cm, sm = cg_ref[...], sg_ref[...], cm_ref[...], sm_ref[...]

    def head_norm_rope(x, gain):
        ms = _split_dot(x * x, same_head) * (1.0 / HEAD_DIM)
        return _rope(x * lax.rsqrt(ms + NORM_EPS) * gain, cg, sg, HEAD_DIM // 4)

    lane = lax.broadcasted_iota(jnp.int32, cg.shape, 1)
    low = lane < HEAD_DIM

    def v_slabs(pair):
        swapped = pltpu.roll(pair, HEAD_DIM, 1)
        return [jnp.where(m, x, 1.0).astype(BF16)
                for m, x in ((low, pair), (~low, pair), (~low, swapped), (low, swapped))]

    scale_g = LOG2E / float(np.sqrt(HEAD_DIM))
    for h in range(GQA_HEADS):
        x = p1[:, h * LANES:(h + 1) * LANES]
        q_ref[0, h] = (head_norm_rope(x, gq_ref[...]) * scale_g).astype(BF16)
    o = GQA_HEADS * LANES
    kt_ref[0, 0, 0] = head_norm_rope(p1[:, o:o + LANES], gk_ref[...]).T.astype(BF16)
    v0_lo, v1_hi, v0_hi, v1_lo = v_slabs(p1[:, o + LANES:o + 2 * LANES])
    v_ref[0, 0], v_ref[0, 1], v_ref[0, 2], v_ref[0, 3] = v0_lo, v0_hi, v1_lo, v1_hi
    k_rope = _rope(p1[:, o + 2 * LANES:o + 3 * LANES], cm, sm, MLA_ROPE // 4)
    o += 3 * LANES
    cq = _rms(p1[:, o:o + MLA_Q_RANK], nq_ref[...]).astype(BF16)
    ckv = _rms(p1[:, o + MLA_Q_RANK:o + MLA_Q_RANK + MLA_KV_RANK], nkv_ref[...]).astype(BF16)

    scale_m = LOG2E / float(np.sqrt(MLA_NOPE + MLA_ROPE))
    qm = jnp.dot(cq, wuq_ref[...], preferred_element_type=F32)
    for h in range(MLA_HEADS):
        x = qm[:, h * LANES:(h + 1) * LANES]
        q_ref[0, GQA_HEADS + h] = (_rope(x, cm, sm, MLA_ROPE // 4) * scale_m).astype(BF16)
    km = jnp.dot(ckv, wuk_ref[...], preferred_element_type=F32)
    for h in range(MLA_HEADS):
        kt_ref[0, 1 + h, 0] = (km[:, h * LANES:(h + 1) * LANES] + k_rope).T.astype(BF16)
    vm = jnp.dot(ckv, wuv_ref[...], preferred_element_type=F32)
    for j in range(MLA_HEADS // 2):
        even_lo, odd_hi, _, _ = v_slabs(vm[:, j * LANES:(j + 1) * LANES])
        v_ref[0, 2 * GQA_KV_HEADS + 2 * j] = even_lo
        v_ref[0, 2 * GQA_KV_HEADS + 2 * j + 1] = odd_hi


def _projection(h, an, w1, wg, wuq, wuk, wuv, gq, gk, nq, nkv, cg, sg, cm, sm, tm):
    nb, lp, d = h.shape
    nt = lp // tm
    full = lambda a: _resident(a)
    tab = pl.BlockSpec((tm, LANES), lambda b, i: (i, 0))
    return pl.pallas_call(
        _proj_kernel,
        grid=(nb, nt),
        in_specs=[pl.BlockSpec((1, tm, d), lambda b, i: (b, i, 0)),
                  full(an), full(w1), full(wg), full(wuq), full(wuk), full(wuv),
                  full(gq), full(gk), full(nq), full(nkv), tab, tab, tab, tab],
        out_specs=[pl.BlockSpec((1, N_HEAD_SLOTS, tm, LANES), lambda b, i: (b, 0, i, 0)),
                   pl.BlockSpec((1, N_K_SLABS, 1, LANES, tm), lambda b, i: (b, 0, i, 0, 0)),
                   pl.BlockSpec((1, N_V_SLABS, tm, LANES), lambda b, i: (b, 0, i, 0)),
                   pl.BlockSpec((1, tm, 2 * d), lambda b, i: (b, i, 0))],
        out_shape=[jax.ShapeDtypeStruct((nb, N_HEAD_SLOTS, lp, LANES), BF16),
                   jax.ShapeDtypeStruct((nb, N_K_SLABS, nt, LANES, tm), BF16),
                   jax.ShapeDtypeStruct((nb, N_V_SLABS, lp, LANES), BF16),
                   jax.ShapeDtypeStruct((nb, lp, 2 * d), BF16)],
        compiler_params=_cparams(("parallel", "parallel")),
        name="projection",
    )(h, an, w1, wg, wuq, wuk, wuv, gq, gk, nq, nkv, cg, sg, cm, sm)


def _attn_kernel(q_ref, kta_ref, ktb_ref, va_ref, vb_ref, o_ref, s0_ref, s1_ref, *,
                 n_blocks, tk, valid_last):
    tq = q_ref.shape[2]
    heads = ((q_ref[0, 0], kta_ref, va_ref), (q_ref[0, 1], ktb_ref, vb_ref))

    def scores(j, s_ref):
        for h, (q, kt_ref, _) in enumerate(heads):
            s_ref[h] = jnp.dot(q, kt_ref[0, 0, j], preferred_element_type=F32)

    def update(j, s_ref, carry, masked=False):
        out = []
        for h, ((_, _, v_ref), (m, acc)) in enumerate(zip(heads, carry)):
            s = s_ref[h]
            if masked:
                col = lax.broadcasted_iota(jnp.int32, s.shape, 1)
                s = jnp.where(col < valid_last, s, NEG_BIG)
            m_new = jnp.maximum(m, jnp.max(s, axis=1, keepdims=True))
            alpha = jnp.exp2(m - m_new)
            p = jnp.exp2(s - m_new).astype(BF16)
            v = v_ref[0, 0, pl.ds(pl.multiple_of(j * tk, tk), tk), :]
            out.append((m_new, alpha * acc + jnp.dot(p, v, preferred_element_type=F32)))
        return tuple(out)

    def pair(i, carry):
        scores(2 * i + 1, s1_ref)
        carry = update(2 * i, s0_ref, carry)
        scores(2 * i + 2, s0_ref)
        return update(2 * i + 1, s1_ref, carry)

    last = n_blocks - 1
    carry = ((jnp.full((tq, 1), NEG_BIG, F32), jnp.zeros((tq, LANES), F32)),) * 2
    scores(0, s0_ref)
    carry = lax.fori_loop(0, last // 2, pair, carry)
    if last % 2:
        scores(last, s1_ref)
        carry = update(last - 1, s0_ref, carry)
        (_, acc_a), (_, acc_b) = update(last, s1_ref, carry, masked=True)
    else:
        (_, acc_a), (_, acc_b) = update(last, s0_ref, carry, masked=True)
    o_a = acc_a / pltpu.roll(acc_a, HEAD_DIM, 1)
    o_b = acc_b / pltpu.roll(acc_b, HEAD_DIM, 1)
    lane = lax.broadcasted_iota(jnp.int32, o_a.shape, 1)
    o_ref[0] = jnp.where(lane < HEAD_DIM, o_a, o_b).astype(BF16)


def _k_slab(h):
    return jnp.where(h < GQA_HEADS, 0, h - (GQA_HEADS - 1))


def _v_slab(h):
    gqa = 2 * (h // (GQA_HEADS // GQA_KV_HEADS)) + h % 2
    return jnp.where(h < GQA_HEADS, gqa, h - GQA_HEADS + 2 * GQA_KV_HEADS)


def _attention(q, kt, v, seq_len, tq):
    nb, _, lp, _ = q.shape
    nt, tk = kt.shape[2], kt.shape[4]
    valid_last = seq_len - (nt - 1) * tk
    assert 0 < valid_last <= tk
    kspec = lambda par: pl.BlockSpec((1, 1, nt, LANES, tk),
                                     lambda b, j, i: (b, _k_slab(2 * j + par), 0, 0, 0))
    vspec = lambda par: pl.BlockSpec((1, 1, lp, LANES),
                                     lambda b, j, i: (b, _v_slab(2 * j + par), 0, 0))
    return pl.pallas_call(
        functools.partial(_attn_kernel, n_blocks=nt, tk=tk, valid_last=valid_last),
        grid=(nb, N_HEAD_SLOTS // 2, lp // tq),
        in_specs=[pl.BlockSpec((1, 2, tq, LANES), lambda b, j, i: (b, j, i, 0)),
                  kspec(0), kspec(1), vspec(0), vspec(1)],
        out_specs=pl.BlockSpec((1, tq, LANES), lambda b, j, i: (b, i, j)),
        out_shape=jax.ShapeDtypeStruct((nb, lp, (N_HEAD_SLOTS // 2) * LANES), BF16),
        scratch_shapes=[pltpu.VMEM((2, tq, tk), F32), pltpu.VMEM((2, tq, tk), F32)],
        compiler_params=_cparams(("parallel", "parallel", "parallel")),
        name="attention",
    )(q, kt, kt, v, v)


def _post_kernel(o_ref, g_ref, h_ref, wog_ref, wom_ref, wout_ref, fn_ref, wrh_ref, wrl_ref,
                 h2_ref, xn_ref, aff_ref, *, seq_len):
    tm = o_ref.shape[1]
    half = o_ref.shape[2] // 2
    o = o_ref[0]
    a = jnp.dot(o[:, :half], wog_ref[...], preferred_element_type=F32)
    b = jnp.dot(o[:, half:], wom_ref[...], preferred_element_type=F32)
    g = g_ref[0].astype(F32)
    mixed = (g[:, :D_MODEL] * a + g[:, D_MODEL:] * b).astype(BF16)
    h2 = h_ref[0] + jnp.dot(mixed, wout_ref[...], preferred_element_type=F32)
    h2_ref[0] = h2
    xn = _rms(h2, fn_ref[...])
    xn_ref[0] = xn.astype(BF16)
    hi = xn.astype(BF16)
    lo = (xn - hi.astype(F32)).astype(BF16)
    logits = (jnp.dot(hi, wrh_ref[...], preferred_element_type=F32)
              + jnp.dot(lo, wrh_ref[...], preferred_element_type=F32)
              + jnp.dot(hi, wrl_ref[...], preferred_element_type=F32))
    lane = lax.broadcasted_iota(jnp.int32, logits.shape, 1)
    logits = jnp.where(lane < N_EXPERTS, logits, NEG_BIG)
    e = jnp.exp(logits - jnp.max(logits, axis=1, keepdims=True))
    aff = e / jnp.sum(e, axis=1, keepdims=True)
    row = pl.program_id(1) * tm + lax.broadcasted_iota(jnp.int32, logits.shape, 0)
    aff = jnp.where(row < seq_len, aff, -1.0)
    aff_ref[...] = aff.T[:N_EXPERTS]


def _post(o, g, h, wog, wom, wout, fn, wrh, wrl, seq_len, tm):
    nb, lp, d = h.shape
    nt = lp // tm
    full = lambda a: _resident(a)
    row = lambda w: pl.BlockSpec((1, tm, w), lambda b, i: (b, i, 0))
    return pl.pallas_call(
        functools.partial(_post_kernel, seq_len=seq_len),
        grid=(nb, nt),
        in_specs=[row(d), row(2 * d), row(d), full(wog), full(wom), full(wout), full(fn),
                  full(wrh), full(wrl)],
        out_specs=[row(d), row(d), pl.BlockSpec((N_EXPERTS, tm), lambda b, i: (0, b * nt + i))],
        out_shape=[jax.ShapeDtypeStruct((nb, lp, d), F32),
                   jax.ShapeDtypeStruct((nb, lp, d), BF16),
                   jax.ShapeDtypeStruct((N_EXPERTS, nb * lp), F32)],
        compiler_params=_cparams(("parallel", "parallel")),
        name="merge_router",
    )(o, g, h, wog, wom, wout, fn, wrh, wrl)


def _ffn_kernel(x_ref, gate_ref, wg_ref, wu_ref, wd_ref, y_ref):
    x = x_ref[0]
    hg = jnp.dot(x, wg_ref[0], preferred_element_type=F32)
    hu = jnp.dot(x, wu_ref[0], preferred_element_type=F32)
    act = (hg * jax.nn.sigmoid(hg) * hu).astype(BF16)
    y_ref[0] = jnp.dot(act, wd_ref[0], preferred_element_type=F32) * gate_ref[0]


def _expert_ffn(xs, gate, wg, wu, wd, tc):
    ne, cp, d = xs.shape
    f = wg.shape[2]
    return pl.pallas_call(
        _ffn_kernel,
        grid=(ne, cp // tc),
        in_specs=[pl.BlockSpec((1, tc, d), lambda e, c: (e, c, 0)),
                  pl.BlockSpec((1, tc, 1), lambda e, c: (e, c, 0)),
                  pl.BlockSpec((1, d, f), lambda e, c: (e, 0, 0)),
                  pl.BlockSpec((1, d, f), lambda e, c: (e, 0, 0)),
                  pl.BlockSpec((1, f, d), lambda e, c: (e, 0, 0))],
        out_specs=pl.BlockSpec((1, tc, d), lambda e, c: (e, c, 0)),
        out_shape=jax.ShapeDtypeStruct((ne, cp, d), F32),
        compiler_params=_cparams(("parallel", "parallel")),
        name="expert_ffn",
    )(xs, gate, wg, wu, wd)


def _final_kernel(h_ref, gain_ref, y_ref):
    y_ref[0] = _rms(h_ref[0], gain_ref[...])


def _final_norm(h, gain, seq, tm):
    nb, lp, d = h.shape
    return pl.pallas_call(
        _final_kernel,
        grid=(nb, seq // tm),
        in_specs=[pl.BlockSpec((1, tm, d), lambda b, i: (b, i, 0)),
                  pl.BlockSpec(gain.shape, lambda b, i: (0, 0))],
        out_specs=pl.BlockSpec((1, tm, d), lambda b, i: (b, i, 0)),
        out_shape=jax.ShapeDtypeStruct((nb, seq, d), F32),
        compiler_params=_cparams(("parallel", "parallel")),
        name="final_norm",
    )(h, gain)


def _rope_tables(seq, lp):
    j = jnp.arange(lp)
    grid_tok = j < seq
    meta_tok = (j >= seq) & (j < seq + N_META)
    row = jnp.where(grid_tok, j // GRID_W, jnp.where(meta_tok, -1, 0)).astype(F32)
    col = jnp.where(grid_tok, j % GRID_W, jnp.where(meta_tok, j - seq, 0)).astype(F32)

    def axis_tables(pos, half):
        inv = ROPE_BASE ** (-jnp.arange(half, dtype=F32) / half)
        ang = pos[:, None] * inv[None, :]
        cos, sin = jnp.cos(ang), jnp.sin(ang)
        return jnp.concatenate([cos, cos], 1), jnp.concatenate([-sin, sin], 1)

    def both_axes(half):
        cr, sr = axis_tables(row, half)
        cc, sc = axis_tables(col, half)
        return jnp.concatenate([cr, cc], 1), jnp.concatenate([sr, sc], 1)

    cg, sg = both_axes(HEAD_DIM // 4)
    cg, sg = jnp.tile(cg, (1, 2)), jnp.tile(sg, (1, 2))
    cm32, sm32 = both_axes(MLA_ROPE // 4)
    ones = jnp.ones((lp, MLA_NOPE), F32)
    zeros = jnp.zeros((lp, MLA_NOPE), F32)
    tail1 = jnp.ones((lp, LANES - MLA_NOPE - MLA_ROPE), F32)
    tail0 = jnp.zeros((lp, LANES - MLA_NOPE - MLA_ROPE), F32)
    cm = jnp.concatenate([ones, cm32, tail1], 1)
    sm = jnp.concatenate([zeros, sm32, tail0], 1)
    return cg, sg, cm, sm


def _prep_weights(w_in, w_uq, w_ukv):
    d = w_in.shape[0]
    splits = np.cumsum([GQA_HEADS * HEAD_DIM, GQA_KV_HEADS * HEAD_DIM, GQA_KV_HEADS * HEAD_DIM,
                        MLA_Q_RANK, MLA_KV_RANK, MLA_ROPE])
    wq, wk, wv, wcq, wckv, wkr, wgate = jnp.split(w_in, splits, axis=1)
    wq = wq.reshape(d, GQA_HEADS, HEAD_DIM)
    zq = jnp.zeros_like(wq)
    rep = GQA_HEADS // GQA_KV_HEADS
    in_g0 = (jnp.arange(GQA_HEADS) < rep)[None, :, None]
    wq = jnp.concatenate([jnp.where(in_g0, wq, zq), jnp.where(in_g0, zq, wq)], axis=2)
    wq = wq.reshape(d, GQA_HEADS * LANES)
    wkr = jnp.pad(wkr, ((0, 0), (MLA_NOPE, LANES - MLA_NOPE - MLA_ROPE)))
    w1 = jnp.concatenate([wq, wk, wv, wkr, wcq, wckv], axis=1).astype(BF16)

    wuq = w_uq.reshape(MLA_Q_RANK, MLA_HEADS, MLA_NOPE + MLA_ROPE)
    wuq = jnp.pad(wuq, ((0, 0), (0, 0), (0, LANES - MLA_NOPE - MLA_ROPE)))
    wuq = wuq.reshape(MLA_Q_RANK, MLA_HEADS * LANES).astype(BF16)
    wukv = w_ukv.reshape(MLA_KV_RANK, MLA_HEADS, MLA_NOPE + MLA_V)
    wuk = jnp.pad(wukv[:, :, :MLA_NOPE], ((0, 0), (0, 0), (0, LANES - MLA_NOPE)))
    wuk = wuk.reshape(MLA_KV_RANK, MLA_HEADS * LANES).astype(BF16)
    wuv = wukv[:, :, MLA_NOPE:].reshape(MLA_KV_RANK, MLA_HEADS * MLA_V).astype(BF16)
    return w1, wgate.astype(BF16), wuq, wuk, wuv


def _token_tile(seq):
    for t in (768, 512, 256):
        lp = -(-(seq + N_META) // t) * t
        if lp - t < seq + N_META:
            return t, lp
    raise ValueError(seq)


def _chunk(cp):
    for t in range(2048, 0, -128):
        if cp % t == 0:
            return t
    raise ValueError(cp)


def kernel(x_prompt, x_sample, meta_tokens, attn_norm, w_in, gqa_q_norm, gqa_k_norm, mla_q_norm,
           mla_kv_norm, mla_w_uq, mla_w_ukv, w_o_gqa, w_o_mla, w_out, ffn_norm, w_router, w_gate,
           w_up, w_down, final_norm):
    seq = x_prompt.shape[1]
    assert x_sample.shape[1] == seq and attn_norm.shape[0] == 1
    d = x_prompt.shape[2]
    seq_len = seq + N_META
    tm, lp = _token_tile(seq)
    groups = (x_prompt, x_sample)
    nbs = [g.shape[0] for g in groups]
    nb = sum(nbs)

    meta = meta_tokens.astype(F32)[None]
    h = jnp.concatenate([
        jnp.concatenate([g, jnp.broadcast_to(meta, (g.shape[0], N_META, d)),
                         jnp.zeros((g.shape[0], lp - seq_len, d), F32)], axis=1)
        for g in groups], axis=0)

    w1, wgate, wuq, wuk, wuv = _prep_weights(w_in[0], mla_w_uq[0], mla_w_ukv[0])
    cg, sg, cm, sm = _rope_tables(seq, lp)
    tile2 = lambda v: jnp.tile(v[0].astype(F32), 2)[None]
    q, kt, v, gates = _projection(
        h, attn_norm[0][None], w1, wgate, wuq, wuk, wuv, tile2(gqa_q_norm), tile2(gqa_k_norm),
        mla_q_norm[0][None], mla_kv_norm[0][None], cg, sg, cm, sm, tm)
    o = _attention(q, kt, v, seq_len, tm)

    wr = jnp.pad(w_router[0], ((0, 0), (0, LANES - N_EXPERTS)))
    wrh = wr.astype(BF16)
    wrl = (wr - wrh.astype(F32)).astype(BF16)
    h2, xn, aff_t = _post(o, gates, h, w_o_gqa[0].astype(BF16), w_o_mla[0].astype(BF16),
                          w_out[0].astype(BF16), ffn_norm[0][None], wrh, wrl, seq_len, tm)

    wg_e, wu_e, wd_e = w_gate[0].astype(BF16), w_up[0].astype(BF16), w_down[0].astype(BF16)
    outs = []
    b0 = 0
    for nbg in nbs:
        n = nbg * seq_len
        cap = (CAPACITY_FACTOR * n) // N_EXPERTS
        cp = -(-cap // LANES) * LANES
        aff_g = aff_t[:, b0 * lp:(b0 + nbg) * lp]
        gate, idx = lax.top_k(aff_g, cap)
        gate = jnp.pad(gate, ((0, 0), (0, cp - cap)))
        idx = jnp.pad(idx, ((0, 0), (0, cp - cap)))
        xg = xn[b0:b0 + nbg].reshape(nbg * lp, d)
        y = _expert_ffn(xg[idx], gate[..., None], wg_e, wu_e, wd_e, _chunk(cp))
        hg = h2[b0:b0 + nbg].reshape(nbg * lp, d)
        hg = hg.at[idx.reshape(-1)].add(y.reshape(-1, d))
        outs.append(_final_norm(hg.reshape(nbg, lp, d), final_norm[None], seq, 512))
        b0 += nbg
    return tuple(outs)
```

```python
import functools

import jax
import jax.numpy as jnp
import numpy as np
from jax import lax
from jax.experimental import pallas as pl
from jax.experimental.pallas import tpu as pltpu

F32 = jnp.float32
BF16 = jnp.bfloat16

D_MODEL = 1024
N_META = 16
GRID_W = 64
ROPE_BASE = 10000.0
NORM_EPS = 1e-6
GQA_HEADS = 8
GQA_KV_HEADS = 2
HEAD_DIM = 64
MLA_HEADS = 8
MLA_NOPE = 64
MLA_ROPE = 32
MLA_V = 64
MLA_Q_RANK = 384
MLA_KV_RANK = 256
N_EXPERTS = 16
CAPACITY_FACTOR = 2
EXPERT_FF = 512
LANES = 128
N_HEAD_SLOTS = GQA_HEADS + MLA_HEADS
N_K_SLABS = 1 + MLA_HEADS
N_V_SLABS = 2 * GQA_KV_HEADS + MLA_HEADS
LOG2E = float(np.log2(np.e))
VMEM_LIMIT = 56 * 1024 * 1024
NEG_BIG = -1e30


def _cparams(sem):
    return pltpu.CompilerParams(dimension_semantics=sem, vmem_limit_bytes=VMEM_LIMIT)


def _resident(a):
    return pl.BlockSpec(a.shape, lambda *_: (0,) * a.ndim, pipeline_mode=pl.Buffered(1))


def _rms(x, gain):
    return x * lax.rsqrt(jnp.mean(x * x, axis=-1, keepdims=True) + NORM_EPS) * gain


def _rope(x, cos, sin_signed, half):
    lane = lax.broadcasted_iota(jnp.int32, x.shape, 1)
    first = (lane % (2 * half)) < half
    partner = jnp.where(first, pltpu.roll(x, LANES - half, 1), pltpu.roll(x, half, 1))
    return x * cos + partner * sin_signed


def _split_dot(x, w_bf16):
    hi = x.astype(BF16)
    lo = (x - hi.astype(F32)).astype(BF16)
    return (jnp.dot(hi, w_bf16, preferred_element_type=F32)
            + jnp.dot(lo, w_bf16, preferred_element_type=F32))


def _proj_kernel(h_ref, an_ref, w1_ref, wg_ref, wuq_ref, wuk_ref, wuv_ref,
                 gq_ref, gk_ref, nq_ref, nkv_ref, cg_ref, sg_ref, cm_ref, sm_ref,
                 q_ref, kt_ref, v_ref, g_ref):
    hn = _rms(h_ref[0], an_ref[...]).astype(BF16)
    p1 = jnp.dot(hn, w1_ref[...], preferred_element_type=F32)
    g_ref[0] = jax.nn.sigmoid(jnp.dot(hn, wg_ref[...], preferred_element_type=F32)).astype(BF16)

    r = lax.broadcasted_iota(jnp.int32, (LANES, LANES), 0) // HEAD_DIM
    c = lax.broadcasted_iota(jnp.int32, (LANES, LANES), 1) // HEAD_DIM
    same_head = (r == c).astype(BF16)
    cg, sg, cm, sm = cg_ref[...], sg_ref[...], cm_ref[...], sm_ref[...]

    def head_norm_rope(x, gain):
        ms = _split_dot(x * x, same_head) * (1.0 / HEAD_DIM)
        return _rope(x * lax.rsqrt(ms + NORM_EPS) * gain, cg, sg, HEAD_DIM // 4)

    lane = lax.broadcasted_iota(jnp.int32, cg.shape, 1)
    low = lane < HEAD_DIM

    def v_slabs(pair):
        swapped = pltpu.roll(pair, HEAD_DIM, 1)
        return [jnp.where(m, x, 1.0).astype(BF16)
                for m, x in ((low, pair), (~low, pair), (~low, swapped), (low, swapped))]

    scale_g = LOG2E / float(np.sqrt(HEAD_DIM))
    for h in range(GQA_HEADS):
        x = p1[:, h * LANES:(h + 1) * LANES]
        q_ref[0, h] = (head_norm_rope(x, gq_ref[...]) * scale_g).astype(BF16)
    o = GQA_HEADS * LANES
    kt_ref[0, 0, 0] = head_norm_rope(p1[:, o:o + LANES], gk_ref[...]).T.astype(BF16)
    v0_lo, v1_hi, v0_hi, v1_lo = v_slabs(p1[:, o + LANES:o + 2 * LANES])
    v_ref[0, 0], v_ref[0, 1], v_ref[0, 2], v_ref[0, 3] = v0_lo, v0_hi, v1_lo, v1_hi
    k_rope = _rope(p1[:, o + 2 * LANES:o + 3 * LANES], cm, sm, MLA_ROPE // 4)
    o += 3 * LANES
    cq = _rms(p1[:, o:o + MLA_Q_RANK], nq_ref[...]).astype(BF16)
    ckv = _rms(p1[:, o + MLA_Q_RANK:o + MLA_Q_RANK + MLA_KV_RANK], nkv_ref[...]).astype(BF16)

    scale_m = LOG2E / float(np.sqrt(MLA_NOPE + MLA_ROPE))
    qm = jnp.dot(cq, wuq_ref[...], preferred_element_type=F32)
    for h in range(MLA_HEADS):
        x = qm[:, h * LANES:(h + 1) * LANES]
        q_ref[0, GQA_HEADS + h] = (_rope(x, cm, sm, MLA_ROPE // 4) * scale_m).astype(BF16)
    km = jnp.dot(ckv, wuk_ref[...], preferred_element_type=F32)
    for h in range(MLA_HEADS):
        kt_ref[0, 1 + h, 0] = (km[:, h * LANES:(h + 1) * LANES] + k_rope).T.astype(BF16)
    vm = jnp.dot(ckv, wuv_ref[...], preferred_element_type=F32)
    for j in range(MLA_HEADS // 2):
        even_lo, odd_hi, _, _ = v_slabs(vm[:, j * LANES:(j + 1) * LANES])
        v_ref[0, 2 * GQA_KV_HEADS + 2 * j] = even_lo
        v_ref[0, 2 * GQA_KV_HEADS + 2 * j + 1] = odd_hi


def _projection(h, an, w1, wg, wuq, wuk, wuv, gq, gk, nq, nkv, cg, sg, cm, sm, tm):
    nb, lp, d = h.shape
    nt = lp // tm
    full = lambda a: _resident(a)
    tab = pl.BlockSpec((tm, LANES), lambda b, i: (i, 0))
    return pl.pallas_call(
        _proj_kernel,
        grid=(nb, nt),
        in_specs=[pl.BlockSpec((1, tm, d), lambda b, i: (b, i, 0)),
                  full(an), full(w1), full(wg), full(wuq), full(wuk), full(wuv),
                  full(gq), full(gk), full(nq), full(nkv), tab, tab, tab, tab],
        out_specs=[pl.BlockSpec((1, N_HEAD_SLOTS, tm, LANES), lambda b, i: (b, 0, i, 0)),
                   pl.BlockSpec((1, N_K_SLABS, 1, LANES, tm), lambda b, i: (b, 0, i, 0, 0)),
                   pl.BlockSpec((1, N_V_SLABS, tm, LANES), lambda b, i: (b, 0, i, 0)),
                   pl.BlockSpec((1, tm, 2 * d), lambda b, i: (b, i, 0))],
        out_shape=[jax.ShapeDtypeStruct((nb, N_HEAD_SLOTS, lp, LANES), BF16),
                   jax.ShapeDtypeStruct((nb, N_K_SLABS, nt, LANES, tm), BF16),
                   jax.ShapeDtypeStruct((nb, N_V_SLABS, lp, LANES), BF16),
                   jax.ShapeDtypeStruct((nb, lp, 2 * d), BF16)],
        compiler_params=_cparams(("parallel", "parallel")),
        name="projection",
    )(h, an, w1, wg, wuq, wuk, wuv, gq, gk, nq, nkv, cg, sg, cm, sm)


def _attn_kernel(q_ref, kta_ref, ktb_ref, va_ref, vb_ref, o_ref, s0_ref, s1_ref, *,
                 n_blocks, tk, valid_last):
    tq = q_ref.shape[2]
    heads = ((q_ref[0, 0], kta_ref, va_ref), (q_ref[0, 1], ktb_ref, vb_ref))

    def scores(j, s_ref):
        for h, (q, kt_ref, _) in enumerate(heads):
            s_ref[h] = jnp.dot(q, kt_ref[0, 0, j], preferred_element_type=F32)

    def update(j, s_ref, carry, masked=False):
        out = []
        for h, ((_, _, v_ref), (m, acc)) in enumerate(zip(heads, carry)):
            s = s_ref[h]
            if masked:
                col = lax.broadcasted_iota(jnp.int32, s.shape, 1)
                s = jnp.where(col < valid_last, s, NEG_BIG)
            m_new = jnp.maximum(m, jnp.max(s, axis=1, keepdims=True))
            alpha = jnp.exp2(m - m_new)
            p = jnp.exp2(s - m_new).astype(BF16)
            v = v_ref[0, 0, pl.ds(pl.multiple_of(j * tk, tk), tk), :]
            out.append((m_new, alpha * acc + jnp.dot(p, v, preferred_element_type=F32)))
        return tuple(out)

    def pair(i, carry):
        scores(2 * i + 1, s1_ref)
        carry = update(2 * i, s0_ref, carry)
        scores(2 * i + 2, s0_ref)
        return update(2 * i + 1, s1_ref, carry)

    last = n_blocks - 1
    carry = ((jnp.full((tq, 1), NEG_BIG, F32), jnp.zeros((tq, LANES), F32)),) * 2
    scores(0, s0_ref)
    carry = lax.fori_loop(0, last // 2, pair, carry)
    if last % 2:
        scores(last, s1_ref)
        carry = update(last - 1, s0_ref, carry)
        (_, acc_a), (_, acc_b) = update(last, s1_ref, carry, masked=True)
    else:
        (_, acc_a), (_, acc_b) = update(last, s0_ref, carry, masked=True)
    o_a = acc_a / pltpu.roll(acc_a, HEAD_DIM, 1)
    o_b = acc_b / pltpu.roll(acc_b, HEAD_DIM, 1)
    lane = lax.broadcasted_iota(jnp.int32, o_a.shape, 1)
    o_ref[0] = jnp.where(lane < HEAD_DIM, o_a, o_b).astype(BF16)


def _k_slab(h):
    return jnp.where(h < GQA_HEADS, 0, h - (GQA_HEADS - 1))


def _v_slab(h):
    gqa = 2 * (h // (GQA_HEADS // GQA_KV_HEADS)) + h % 2
    return jnp.where(h < GQA_HEADS, gqa, h - GQA_HEADS + 2 * GQA_KV_HEADS)


def _attention(q, kt, v, seq_len, tq):
    nb, _, lp, _ = q.shape
    nt, tk = kt.shape[2], kt.shape[4]
    valid_last = seq_len - (nt - 1) * tk
    assert 0 < valid_last <= tk
    kspec = lambda par: pl.BlockSpec((1, 1, nt, LANES, tk),
                                     lambda b, j, i: (b, _k_slab(2 * j + par), 0, 0, 0))
    vspec = lambda par: pl.BlockSpec((1, 1, lp, LANES),
                                     lambda b, j, i: (b, _v_slab(2 * j + par), 0, 0))
    return pl.pallas_call(
        functools.partial(_attn_kernel, n_blocks=nt, tk=tk, valid_last=valid_last),
        grid=(nb, N_HEAD_SLOTS // 2, lp // tq),
        in_specs=[pl.BlockSpec((1, 2, tq, LANES), lambda b, j, i: (b, j, i, 0)),
                  kspec(0), kspec(1), vspec(0), vspec(1)],
        out_specs=pl.BlockSpec((1, tq, LANES), lambda b, j, i: (b, i, j)),
        out_shape=jax.ShapeDtypeStruct((nb, lp, (N_HEAD_SLOTS // 2) * LANES), BF16),
        scratch_shapes=[pltpu.VMEM((2, tq, tk), F32), pltpu.VMEM((2, tq, tk), F32)],
        compiler_params=_cparams(("parallel", "parallel", "parallel")),
        name="attention",
    )(q, kt, kt, v, v)


def _post_kernel(o_ref, g_ref, h_ref, wog_ref, wom_ref, wout_ref, fn_ref, wrh_ref, wrl_ref,
                 h2_ref, xn_ref, aff_ref, *, seq_len):
    tm = o_ref.shape[1]
    half = o_ref.shape[2] // 2
    o = o_ref[0]
    a = jnp.dot(o[:, :half], wog_ref[...], preferred_element_type=F32)
    b = jnp.dot(o[:, half:], wom_ref[...], preferred_element_type=F32)
    g = g_ref[0].astype(F32)
    mixed = (g[:, :D_MODEL] * a + g[:, D_MODEL:] * b).astype(BF16)
    h2 = h_ref[0] + jnp.dot(mixed, wout_ref[...], preferred_element_type=F32)
    h2_ref[0] = h2
    xn = _rms(h2, fn_ref[...])
    xn_ref[0] = xn.astype(BF16)
    hi = xn.astype(BF16)
    lo = (xn - hi.astype(F32)).astype(BF16)
    logits = (jnp.dot(hi, wrh_ref[...], preferred_element_type=F32)
              + jnp.dot(lo, wrh_ref[...], preferred_element_type=F32)
              + jnp.dot(hi, wrl_ref[...], preferred_element_type=F32))
    lane = lax.broadcasted_iota(jnp.int32, logits.shape, 1)
    logits = jnp.where(lane < N_EXPERTS, logits, NEG_BIG)
    e = jnp.exp(logits - jnp.max(logits, axis=1, keepdims=True))
    aff = e / jnp.sum(e, axis=1, keepdims=True)
    row = pl.program_id(1) * tm + lax.broadcasted_iota(jnp.int32, logits.shape, 0)
    aff = jnp.where(row < seq_len, aff, -1.0)
    aff_ref[...] = aff.T[:N_EXPERTS]


def _post(o, g, h, wog, wom, wout, fn, wrh, wrl, seq_len, tm):
    nb, lp, d = h.shape
    nt = lp // tm
    full = lambda a: _resident(a)
    row = lambda w: pl.BlockSpec((1, tm, w), lambda b, i: (b, i, 0))
    return pl.pallas_call(
        functools.partial(_post_kernel, seq_len=seq_len),
        grid=(nb, nt),
        in_specs=[row(d), row(2 * d), row(d), full(wog), full(wom), full(wout), full(fn),
                  full(wrh), full(wrl)],
        out_specs=[row(d), row(d), pl.BlockSpec((N_EXPERTS, tm), lambda b, i: (0, b * nt + i))],
        out_shape=[jax.ShapeDtypeStruct((nb, lp, d), F32),
                   jax.ShapeDtypeStruct((nb, lp, d), BF16),
                   jax.ShapeDtypeStruct((N_EXPERTS, nb * lp), F32)],
        compiler_params=_cparams(("parallel", "parallel")),
        name="merge_router",
    )(o, g, h, wog, wom, wout, fn, wrh, wrl)


def _ffn_kernel(x_ref, wg_ref, wu_ref, wd_ref, y_ref):
    x = x_ref[0]
    hg = jnp.dot(x, wg_ref[0], preferred_element_type=F32)
    hu = jnp.dot(x, wu_ref[0], preferred_element_type=F32)
    act = (hg * jax.nn.sigmoid(hg) * hu).astype(BF16)
    y_ref[0] = jnp.dot(act, wd_ref[0], preferred_element_type=F32).astype(BF16)


def _expert_ffn(xs, wg, wu, wd, tc):
    ne, cp, d = xs.shape
    f = wg.shape[2]
    return pl.pallas_call(
        _ffn_kernel,
        grid=(ne, cp // tc),
        in_specs=[pl.BlockSpec((1, tc, d), lambda e, c: (e, c, 0)),
                  pl.BlockSpec((1, d, f), lambda e, c: (e, 0, 0)),
                  pl.BlockSpec((1, d, f), lambda e, c: (e, 0, 0)),
                  pl.BlockSpec((1, f, d), lambda e, c: (e, 0, 0))],
        out_specs=pl.BlockSpec((1, tc, d), lambda e, c: (e, c, 0)),
        out_shape=jax.ShapeDtypeStruct((ne, cp, d), BF16),
        compiler_params=_cparams(("parallel", "parallel")),
        name="expert_ffn",
    )(xs, wg, wu, wd)


RT = 256
SLOT_CHUNK = 32
MCHUNK = 256
STAGE_ROWS = -(-(N_EXPERTS * (RT + 2 * (SLOT_CHUNK - 1))) // MCHUNK) * MCHUNK


def _select_kernel(aff_ref, ridx_ref, gsel_ref, *, cap, idx_bits):
    aff = aff_ref[...]
    bits = lax.bitcast_convert_type(aff, jnp.int32)
    ridx = ridx_ref[...]

    def count(pred):
        return jnp.sum(jnp.where(pred, 1.0, 0.0), axis=1, keepdims=True)

    def value_bit(i, prefix):
        cand = prefix | jnp.left_shift(jnp.int32(1), 30 - i)
        return jnp.where(count(bits >= cand) >= cap, cand, prefix)

    thr = lax.fori_loop(0, 31, value_bit, jnp.zeros((N_EXPERTS, 1), jnp.int32))
    above = bits > thr
    tied = bits == thr
    need = cap - count(above)

    def index_bit(i, prefix):
        cand = prefix | jnp.left_shift(jnp.int32(1), idx_bits - 1 - i)
        return jnp.where(count(tied & (ridx < cand)) < need, cand, prefix)

    last = lax.fori_loop(0, idx_bits, index_bit, jnp.zeros((N_EXPERTS, 1), jnp.int32))
    gsel_ref[...] = jnp.where(above | (tied & (ridx <= last)), aff, -1.0)


def _select(aff, ridx, cap, idx_bits):
    return pl.pallas_call(
        functools.partial(_select_kernel, cap=cap, idx_bits=idx_bits),
        out_shape=jax.ShapeDtypeStruct(aff.shape, F32),
        compiler_params=pltpu.CompilerParams(vmem_limit_bytes=VMEM_LIMIT),
        name="select",
    )(aff, ridx)


def _rank_kernel(gsel_ref, pos_ref, toff_ref, off_ref):
    @pl.when(pl.program_id(0) == 0)
    def _():
        off_ref[...] = jnp.zeros_like(off_ref)

    picked = gsel_ref[...] >= 0.0
    ones = jnp.where(picked, 1.0, 0.0)
    r = lax.broadcasted_iota(jnp.int32, (RT, RT), 0)
    c = lax.broadcasted_iota(jnp.int32, (RT, RT), 1)
    before = jnp.dot(ones.astype(BF16), (r < c).astype(BF16), preferred_element_type=F32)
    off = off_ref[...]
    toff_ref[0] = off.astype(jnp.int32)
    pos_ref[...] = jnp.where(picked, off[:, :1] + before, -1.0).astype(jnp.int32)
    off_ref[...] = off + jnp.sum(ones, axis=1, keepdims=True)


def _rank(gsel):
    ne, n = gsel.shape
    nt = n // RT
    return pl.pallas_call(
        _rank_kernel,
        grid=(nt,),
        in_specs=[pl.BlockSpec((ne, RT), lambda j: (0, j))],
        out_specs=[pl.BlockSpec((ne, RT), lambda j: (0, j)),
                   pl.BlockSpec((1, ne, LANES), lambda j: (j, 0, 0))],
        out_shape=[jax.ShapeDtypeStruct((ne, n), jnp.int32),
                   jax.ShapeDtypeStruct((nt, ne, LANES), jnp.int32)],
        scratch_shapes=[pltpu.VMEM((ne, LANES), F32)],
        compiler_params=_cparams(("arbitrary",)),
        name="rank",
    )(gsel)


def _tile_ranges(toff_ref, j):
    out = []
    for e in range(N_EXPERTS):
        s0 = toff_ref[j * N_EXPERTS + e]
        s1 = toff_ref[(j + 1) * N_EXPERTS + e]
        s0al = (s0 // SLOT_CHUNK) * SLOT_CHUNK
        out.append((s0al, s1 - s0al, s1 - s0))
    return out


def _dispatch_kernel(toff_ref, x_ref, pos_ref, xs_ref, stage_ref, carry_ref, zero_ref, sem, cnt_ref,
                     *, cap, cp):
    j = pl.program_id(0)
    nt = pl.num_programs(0)
    slot = j % 2
    chunk = lambda ref, start: ref.at[pl.ds(pl.multiple_of(start, SLOT_CHUNK), SLOT_CHUNK)]

    def wait_chunks(n, s):
        def body(_, c):
            pltpu.make_async_copy(chunk(zero_ref, 0), chunk(xs_ref.at[0], 0), sem.at[s]).wait()
            return c
        lax.fori_loop(0, n, body, 0)

    @pl.when(j == 0)
    def _():
        carry_ref[...] = jnp.zeros_like(carry_ref)
        zero_ref[...] = jnp.zeros_like(zero_ref)
        cnt_ref[0] = 0
        cnt_ref[1] = 0

    wait_chunks(cnt_ref[slot], slot)

    ranges = _tile_ranges(toff_ref, j)
    pos = pos_ref[...]
    moffs, keys = [], []
    moff = 0
    for e, (s0al, span, _) in enumerate(ranges):
        moffs.append(moff)
        keys.append(jnp.where(pos[e:e + 1] >= 0, pos[e:e + 1] + (moff - s0al), -1))
        moff = moff + ((span + SLOT_CHUNK - 1) // SLOT_CHUNK) * SLOT_CHUNK
    x = x_ref[...]
    stage = stage_ref.at[slot]

    def compact(mc, c):
        row = mc * MCHUNK + lax.broadcasted_iota(jnp.int32, (MCHUNK, RT), 0)
        hit = keys[0] == row
        for k in keys[1:]:
            hit = hit | (k == row)
        rows = jnp.dot(jnp.where(hit, 1.0, 0.0).astype(BF16), x, preferred_element_type=F32)
        stage[pl.ds(pl.multiple_of(mc * MCHUNK, MCHUNK), MCHUNK), :] = rows.astype(BF16)
        return c
    lax.fori_loop(0, (moff + MCHUNK - 1) // MCHUNK, compact, 0)

    sent = 0
    for e, (s0al, span, _) in enumerate(ranges):
        nfull = span // SLOT_CHUNK
        carry = chunk(carry_ref, e * SLOT_CHUNK)

        @pl.when(span > 0)
        def _():
            head = chunk(stage, moffs[e])
            head[...] = head[...] + carry[...]

            def send(k, c):
                pltpu.make_async_copy(chunk(stage, moffs[e] + k * SLOT_CHUNK),
                                      chunk(xs_ref.at[e], s0al + k * SLOT_CHUNK), sem.at[slot]).start()
                return c
            lax.fori_loop(0, nfull, send, 0)
            tail = chunk(stage, moffs[e] + nfull * SLOT_CHUNK)
            keep = span - nfull * SLOT_CHUNK > 0
            carry[...] = jnp.where(keep, tail[...], jnp.zeros_like(tail[...]))
        sent = sent + jnp.where(span > 0, nfull, 0)
    cnt_ref[slot] = sent

    @pl.when(j == nt - 1)
    def _():
        first = (cap // SLOT_CHUNK) * SLOT_CHUNK
        n_tail = (cp - first) // SLOT_CHUNK
        for e in range(N_EXPERTS):
            for k in range(n_tail):
                src = chunk(carry_ref, e * SLOT_CHUNK) if k == 0 else chunk(zero_ref, 0)
                pltpu.make_async_copy(src, chunk(xs_ref.at[e], first + k * SLOT_CHUNK), sem.at[slot]).start()
        wait_chunks(cnt_ref[slot] + N_EXPERTS * n_tail, slot)
        wait_chunks(cnt_ref[1 - slot], 1 - slot)


def _dispatch(toff, xn, pos, tile0, cap, cp):
    ne, n = pos.shape
    d = xn.shape[1]
    grid_spec = pltpu.PrefetchScalarGridSpec(
        num_scalar_prefetch=1,
        grid=(n // RT,),
        in_specs=[pl.BlockSpec((RT, d), lambda j, toff: (tile0 + j, 0)),
                  pl.BlockSpec((ne, RT), lambda j, toff: (0, j))],
        out_specs=pl.BlockSpec(memory_space=pl.ANY),
        scratch_shapes=[pltpu.VMEM((2, STAGE_ROWS, d), BF16),
                        pltpu.VMEM((ne * SLOT_CHUNK, d), BF16),
                        pltpu.VMEM((SLOT_CHUNK, d), BF16),
                        pltpu.SemaphoreType.DMA((2,)),
                        pltpu.SMEM((2,), jnp.int32)])
    return pl.pallas_call(
        functools.partial(_dispatch_kernel, cap=cap, cp=cp),
        grid_spec=grid_spec,
        out_shape=jax.ShapeDtypeStruct((ne, cp, d), BF16),
        compiler_params=_cparams(("arbitrary",)),
        name="dispatch",
    )(toff, xn, pos)


def _combine_kernel(toff_ref, h_ref, pos_ref, gsel_ref, gain_ref, y_ref, o_ref, ycat_ref, sem,
                    *, tiles_per_seq, out_tiles_per_seq):
    g = pl.program_id(0)

    @pl.when(g == 0)
    def _():
        ycat_ref[...] = jnp.zeros_like(ycat_ref)

    @pl.when(g % tiles_per_seq < out_tiles_per_seq)
    def _():
        ranges = _tile_ranges(toff_ref, g)
        chunk = lambda ref, start: ref.at[pl.ds(pl.multiple_of(start, SLOT_CHUNK), SLOT_CHUNK)]
        erow = lax.broadcasted_iota(jnp.int32, (N_EXPERTS, RT), 0)
        keyoff = jnp.zeros((N_EXPERTS, RT), jnp.int32)
        moff = 0
        total = 0
        for e, (s0al, span, picked) in enumerate(ranges):
            nch = jnp.where(picked > 0, (span + SLOT_CHUNK - 1) // SLOT_CHUNK, 0)

            def fetch(k, c, e=e, s0al=s0al, moff=moff):
                pltpu.make_async_copy(chunk(y_ref.at[e], s0al + k * SLOT_CHUNK),
                                      chunk(ycat_ref, moff + k * SLOT_CHUNK), sem.at[0]).start()
                return c
            lax.fori_loop(0, nch, fetch, 0)
            keyoff = jnp.where(erow == e, moff - s0al, keyoff)
            moff = moff + nch * SLOT_CHUNK
            total = total + nch

        pos = pos_ref[...]
        pad_rows = jnp.zeros((LANES - N_EXPERTS, RT), F32)
        key = jnp.where(pos >= 0, pos + keyoff, -1).astype(F32)
        key_t = jnp.concatenate([key, pad_rows], axis=0).T
        gate_t = jnp.concatenate([gsel_ref[...], pad_rows], axis=0).T

        def wait(_, c):
            pltpu.make_async_copy(chunk(y_ref.at[0], 0), chunk(ycat_ref, 0), sem.at[0]).wait()
            return c
        lax.fori_loop(0, total, wait, 0)

        def expand(mc, acc):
            col = (mc * MCHUNK + lax.broadcasted_iota(jnp.int32, (RT, MCHUNK), 1)).astype(F32)
            w = jnp.zeros((RT, MCHUNK), F32)
            for e in range(N_EXPERTS):
                w = jnp.where(key_t[:, e:e + 1] == col, gate_t[:, e:e + 1], w)
            return acc + _split_dot(w, ycat_ref[pl.ds(pl.multiple_of(mc * MCHUNK, MCHUNK), MCHUNK), :])

        acc = lax.fori_loop(0, (moff + MCHUNK - 1) // MCHUNK, expand, h_ref[...])
        o_ref[...] = _rms(acc, gain_ref[...])


def _combine(toff, h2, pos, gsel, gain, y, tile0, lp, seq):
    ne, n = pos.shape
    d = h2.shape[1]
    tiles_per_seq, out_tiles_per_seq = lp // RT, seq // RT
    nbg = n // lp

    def out_index(g, toff):
        return ((g // tiles_per_seq) * out_tiles_per_seq
                + jnp.minimum(g % tiles_per_seq, out_tiles_per_seq - 1), 0)

    grid_spec = pltpu.PrefetchScalarGridSpec(
        num_scalar_prefetch=1,
        grid=(n // RT,),
        in_specs=[pl.BlockSpec((RT, d), lambda g, toff: (tile0 + g, 0)),
                  pl.BlockSpec((ne, RT), lambda g, toff: (0, g)),
                  pl.BlockSpec((ne, RT), lambda g, toff: (0, g)),
                  pl.BlockSpec((1, d), lambda g, toff: (0, 0)),
                  pl.BlockSpec(memory_space=pl.ANY)],
        out_specs=pl.BlockSpec((RT, d), out_index),
        scratch_shapes=[pltpu.VMEM((STAGE_ROWS, d), BF16), pltpu.SemaphoreType.DMA((1,))])
    return pl.pallas_call(
        functools.partial(_combine_kernel, tiles_per_seq=tiles_per_seq,
                          out_tiles_per_seq=out_tiles_per_seq),
        grid_spec=grid_spec,
        out_shape=jax.ShapeDtypeStruct((nbg * seq, d), F32),
        compiler_params=_cparams(("arbitrary",)),
        name="combine",
    )(toff, h2, pos, gsel, gain, y)


def _rope_tables(seq, lp):
    j = jnp.arange(lp)
    grid_tok = j < seq
    meta_tok = (j >= seq) & (j < seq + N_META)
    row = jnp.where(grid_tok, j // GRID_W, jnp.where(meta_tok, -1, 0)).astype(F32)
    col = jnp.where(grid_tok, j % GRID_W, jnp.where(meta_tok, j - seq, 0)).astype(F32)

    def axis_tables(pos, half):
        inv = ROPE_BASE ** (-jnp.arange(half, dtype=F32) / half)
        ang = pos[:, None] * inv[None, :]
        cos, sin = jnp.cos(ang), jnp.sin(ang)
        return jnp.concatenate([cos, cos], 1), jnp.concatenate([-sin, sin], 1)

    def both_axes(half):
        cr, sr = axis_tables(row, half)
        cc, sc = axis_tables(col, half)
        return jnp.concatenate([cr, cc], 1), jnp.concatenate([sr, sc], 1)

    cg, sg = both_axes(HEAD_DIM // 4)
    cg, sg = jnp.tile(cg, (1, 2)), jnp.tile(sg, (1, 2))
    cm32, sm32 = both_axes(MLA_ROPE // 4)
    ones = jnp.ones((lp, MLA_NOPE), F32)
    zeros = jnp.zeros((lp, MLA_NOPE), F32)
    tail1 = jnp.ones((lp, LANES - MLA_NOPE - MLA_ROPE), F32)
    tail0 = jnp.zeros((lp, LANES - MLA_NOPE - MLA_ROPE), F32)
    cm = jnp.concatenate([ones, cm32, tail1], 1)
    sm = jnp.concatenate([zeros, sm32, tail0], 1)
    return cg, sg, cm, sm


def _prep_weights(w_in, w_uq, w_ukv):
    d = w_in.shape[0]
    splits = np.cumsum([GQA_HEADS * HEAD_DIM, GQA_KV_HEADS * HEAD_DIM, GQA_KV_HEADS * HEAD_DIM,
                        MLA_Q_RANK, MLA_KV_RANK, MLA_ROPE])
    wq, wk, wv, wcq, wckv, wkr, wgate = jnp.split(w_in, splits, axis=1)
    wq = wq.reshape(d, GQA_HEADS, HEAD_DIM)
    zq = jnp.zeros_like(wq)
    rep = GQA_HEADS // GQA_KV_HEADS
    in_g0 = (jnp.arange(GQA_HEADS) < rep)[None, :, None]
    wq = jnp.concatenate([jnp.where(in_g0, wq, zq), jnp.where(in_g0, zq, wq)], axis=2)
    wq = wq.reshape(d, GQA_HEADS * LANES)
    wkr = jnp.pad(wkr, ((0, 0), (MLA_NOPE, LANES - MLA_NOPE - MLA_ROPE)))
    w1 = jnp.concatenate([wq, wk, wv, wkr, wcq, wckv], axis=1).astype(BF16)

    wuq = w_uq.reshape(MLA_Q_RANK, MLA_HEADS, MLA_NOPE + MLA_ROPE)
    wuq = jnp.pad(wuq, ((0, 0), (0, 0), (0, LANES - MLA_NOPE - MLA_ROPE)))
    wuq = wuq.reshape(MLA_Q_RANK, MLA_HEADS * LANES).astype(BF16)
    wukv = w_ukv.reshape(MLA_KV_RANK, MLA_HEADS, MLA_NOPE + MLA_V)
    wuk = jnp.pad(wukv[:, :, :MLA_NOPE], ((0, 0), (0, 0), (0, LANES - MLA_NOPE)))
    wuk = wuk.reshape(MLA_KV_RANK, MLA_HEADS * LANES).astype(BF16)
    wuv = wukv[:, :, MLA_NOPE:].reshape(MLA_KV_RANK, MLA_HEADS * MLA_V).astype(BF16)
    return w1, wgate.astype(BF16), wuq, wuk, wuv


def _token_tile(seq):
    for t in (768, 512, 256):
        lp = -(-(seq + N_META) // t) * t
        if lp - t < seq + N_META:
            return t, lp
    raise ValueError(seq)


def _chunk(cp):
    for t in range(2048, 0, -128):
        if cp % t == 0:
            return t
    raise ValueError(cp)


def kernel(x_prompt, x_sample, meta_tokens, attn_norm, w_in, gqa_q_norm, gqa_k_norm, mla_q_norm,
           mla_kv_norm, mla_w_uq, mla_w_ukv, w_o_gqa, w_o_mla, w_out, ffn_norm, w_router, w_gate,
           w_up, w_down, final_norm):
    seq = x_prompt.shape[1]
    assert x_sample.shape[1] == seq and attn_norm.shape[0] == 1
    d = x_prompt.shape[2]
    seq_len = seq + N_META
    tm, lp = _token_tile(seq)
    groups = (x_prompt, x_sample)
    nbs = [g.shape[0] for g in groups]
    nb = sum(nbs)

    meta = meta_tokens.astype(F32)[None]
    h = jnp.concatenate([
        jnp.concatenate([g, jnp.broadcast_to(meta, (g.shape[0], N_META, d)),
                         jnp.zeros((g.shape[0], lp - seq_len, d), F32)], axis=1)
        for g in groups], axis=0)

    w1, wgate, wuq, wuk, wuv = _prep_weights(w_in[0], mla_w_uq[0], mla_w_ukv[0])
    cg, sg, cm, sm = _rope_tables(seq, lp)
    tile2 = lambda v: jnp.tile(v[0].astype(F32), 2)[None]
    q, kt, v, gates = _projection(
        h, attn_norm[0][None], w1, wgate, wuq, wuk, wuv, tile2(gqa_q_norm), tile2(gqa_k_norm),
        mla_q_norm[0][None], mla_kv_norm[0][None], cg, sg, cm, sm, tm)
    o = _attention(q, kt, v, seq_len, tm)

    wr = jnp.pad(w_router[0], ((0, 0), (0, LANES - N_EXPERTS)))
    wrh = wr.astype(BF16)
    wrl = (wr - wrh.astype(F32)).astype(BF16)
    h2, xn, aff_t = _post(o, gates, h, w_o_gqa[0].astype(BF16), w_o_mla[0].astype(BF16),
                          w_out[0].astype(BF16), ffn_norm[0][None], wrh, wrl, seq_len, tm)

    wg_e, wu_e, wd_e = w_gate[0].astype(BF16), w_up[0].astype(BF16), w_down[0].astype(BF16)
    assert seq % RT == 0 and lp % RT == 0
    h2f, xnf = h2.reshape(nb * lp, d), xn.reshape(nb * lp, d)
    outs = []
    b0 = 0
    for nbg in nbs:
        n = nbg * seq_len
        cap = (CAPACITY_FACTOR * n) // N_EXPERTS
        cp = -(-cap // LANES) * LANES
        tile0 = b0 * lp // RT
        j = jnp.arange(nbg * lp, dtype=jnp.int32)
        l = j % lp
        ridx = jnp.where(l < seq_len, (j // lp) * seq_len + jnp.where(l < seq, l + N_META, l - seq), 0)
        gsel = _select(aff_t[:, b0 * lp:(b0 + nbg) * lp], ridx[None], cap, max(n - 1, 1).bit_length())
        pos, tile_off = _rank(gsel)
        toff = jnp.concatenate([tile_off[:, :, 0], jnp.full((1, N_EXPERTS), cap, jnp.int32)]).reshape(-1)
        xs = _dispatch(toff, xnf, pos, tile0, cap, cp)
        y = _expert_ffn(xs, wg_e, wu_e, wd_e, _chunk(cp))
        out = _combine(toff, h2f, pos, gsel, final_norm[None], y, tile0, lp, seq)
        outs.append(out.reshape(nbg, seq, d))
        b0 += nbg
    return tuple(outs)
```

```python
import functools

import jax
import jax.numpy as jnp
import numpy as np
from jax import lax
from jax.experimental import pallas as pl
from jax.experimental.pallas import tpu as pltpu

F32 = jnp.float32
BF16 = jnp.bfloat16

D_MODEL = 1024
N_META = 16
GRID_W = 64
ROPE_BASE = 10000.0
NORM_EPS = 1e-6
GQA_HEADS = 8
GQA_KV_HEADS = 2
HEAD_DIM = 64
MLA_HEADS = 8
MLA_NOPE = 64
MLA_ROPE = 32
MLA_V = 64
MLA_Q_RANK = 384
MLA_KV_RANK = 256
N_EXPERTS = 16
CAPACITY_FACTOR = 2
EXPERT_FF = 512
LANES = 128
N_HEAD_SLOTS = GQA_HEADS + MLA_HEADS
N_K_SLABS = 1 + MLA_HEADS
N_V_SLABS = 2 * GQA_KV_HEADS + MLA_HEADS
LOG2E = float(np.log2(np.e))
VMEM_LIMIT = 56 * 1024 * 1024
NEG_BIG = -1e30


def _cparams(sem):
    return pltpu.CompilerParams(dimension_semantics=sem, vmem_limit_bytes=VMEM_LIMIT)


def _resident(a):
    return pl.BlockSpec(a.shape, lambda *_: (0,) * a.ndim, pipeline_mode=pl.Buffered(1))


def _rms(x, gain):
    return x * lax.rsqrt(jnp.mean(x * x, axis=-1, keepdims=True) + NORM_EPS) * gain


def _rope(x, cos, sin_signed, half):
    lane = lax.broadcasted_iota(jnp.int32, x.shape, 1)
    first = (lane % (2 * half)) < half
    partner = jnp.where(first, pltpu.roll(x, LANES - half, 1), pltpu.roll(x, half, 1))
    return x * cos + partner * sin_signed


def _split_dot(x, w_bf16):
    hi = x.astype(BF16)
    lo = (x - hi.astype(F32)).astype(BF16)
    return (jnp.dot(hi, w_bf16, preferred_element_type=F32)
            + jnp.dot(lo, w_bf16, preferred_element_type=F32))


def _proj_kernel(h_ref, an_ref, w1_ref, wg_ref, wuq_ref, wuk_ref, wuv_ref,
                 gq_ref, gk_ref, nq_ref, nkv_ref, cg_ref, sg_ref, cm_ref, sm_ref,
                 q_ref, kt_ref, v_ref, g_ref):
    hn = _rms(h_ref[0], an_ref[...]).astype(BF16)
    p1 = jnp.dot(hn, w1_ref[...], preferred_element_type=F32)
    g_ref[0] = jax.nn.sigmoid(jnp.dot(hn, wg_ref[...], preferred_element_type=F32)).astype(BF16)

    r = lax.broadcasted_iota(jnp.int32, (LANES, LANES), 0) // HEAD_DIM
    c = lax.broadcasted_iota(jnp.int32, (LANES, LANES), 1) // HEAD_DIM
    same_head = (r == c).astype(BF16)
    cg, sg, cm, sm = cg_ref[...], sg_ref[...], cm_ref[...], sm_ref[...]

    def head_norm_rope(x, gain):
        ms = _split_dot(x * x, same_head) * (1.0 / HEAD_DIM)
        return _rope(x * lax.rsqrt(ms + NORM_EPS) * gain, cg, sg, HEAD_DIM // 4)

    lane = lax.broadcasted_iota(jnp.int32, cg.shape, 1)
    low = lane < HEAD_DIM

    def v_slabs(pair):
        swapped = pltpu.roll(pair, HEAD_DIM, 1)
        return [jnp.where(m, x, 1.0).astype(BF16)
                for m, x in ((low, pair), (~low, pair), (~low, swapped), (low, swapped))]

    scale_g = LOG2E / float(np.sqrt(HEAD_DIM))
    for h in range(GQA_HEADS):
        x = p1[:, h * LANES:(h + 1) * LANES]
        q_ref[0, h] = (head_norm_rope(x, gq_ref[...]) * scale_g).astype(BF16)
    o = GQA_HEADS * LANES
    kt_ref[0, 0, 0] = head_norm_rope(p1[:, o:o + LANES], gk_ref[...]).T.astype(BF16)
    v0_lo, v1_hi, v0_hi, v1_lo = v_slabs(p1[:, o + LANES:o + 2 * LANES])
    v_ref[0, 0], v_ref[0, 1], v_ref[0, 2], v_ref[0, 3] = v0_lo, v0_hi, v1_lo, v1_hi
    k_rope = _rope(p1[:, o + 2 * LANES:o + 3 * LANES], cm, sm, MLA_ROPE // 4)
    o += 3 * LANES
    cq = _rms(p1[:, o:o + MLA_Q_RANK], nq_ref[...]).astype(BF16)
    ckv = _rms(p1[:, o + MLA_Q_RANK:o + MLA_Q_RANK + MLA_KV_RANK], nkv_ref[...]).astype(BF16)

    scale_m = LOG2E / float(np.sqrt(MLA_NOPE + MLA_ROPE))
    qm = jnp.dot(cq, wuq_ref[...], preferred_element_type=F32)
    for h in range(MLA_HEADS):
        x = qm[:, h * LANES:(h + 1) * LANES]
        q_ref[0, GQA_HEADS + h] = (_rope(x, cm, sm, MLA_ROPE // 4) * scale_m).astype(BF16)
    km = jnp.dot(ckv, wuk_ref[...], preferred_element_type=F32)
    for h in range(MLA_HEADS):
        kt_ref[0, 1 + h, 0] = (km[:, h * LANES:(h + 1) * LANES] + k_rope).T.astype(BF16)
    vm = jnp.dot(ckv, wuv_ref[...], preferred_element_type=F32)
    for j in range(MLA_HEADS // 2):
        even_lo, odd_hi, _, _ = v_slabs(vm[:, j * LANES:(j + 1) * LANES])
        v_ref[0, 2 * GQA_KV_HEADS + 2 * j] = even_lo
        v_ref[0, 2 * GQA_KV_HEADS + 2 * j + 1] = odd_hi


def _projection(h, an, w1, wg, wuq, wuk, wuv, gq, gk, nq, nkv, cg, sg, cm, sm, tm):
    nb, lp, d = h.shape
    nt = lp // tm
    full = lambda a: _resident(a)
    tab = pl.BlockSpec((tm, LANES), lambda b, i: (i, 0))
    return pl.pallas_call(
        _proj_kernel,
        grid=(nb, nt),
        in_specs=[pl.BlockSpec((1, tm, d), lambda b, i: (b, i, 0)),
                  full(an), full(w1), full(wg), full(wuq), full(wuk), full(wuv),
                  full(gq), full(gk), full(nq), full(nkv), tab, tab, tab, tab],
        out_specs=[pl.BlockSpec((1, N_HEAD_SLOTS, tm, LANES), lambda b, i: (b, 0, i, 0)),
                   pl.BlockSpec((1, N_K_SLABS, 1, LANES, tm), lambda b, i: (b, 0, i, 0, 0)),
                   pl.BlockSpec((1, N_V_SLABS, tm, LANES), lambda b, i: (b, 0, i, 0)),
                   pl.BlockSpec((1, tm, 2 * d), lambda b, i: (b, i, 0))],
        out_shape=[jax.ShapeDtypeStruct((nb, N_HEAD_SLOTS, lp, LANES), BF16),
                   jax.ShapeDtypeStruct((nb, N_K_SLABS, nt, LANES, tm), BF16),
                   jax.ShapeDtypeStruct((nb, N_V_SLABS, lp, LANES), BF16),
                   jax.ShapeDtypeStruct((nb, lp, 2 * d), BF16)],
        compiler_params=_cparams(("parallel", "parallel")),
        name="projection",
    )(h, an, w1, wg, wuq, wuk, wuv, gq, gk, nq, nkv, cg, sg, cm, sm)


def _attn_kernel(q_ref, kta_ref, ktb_ref, va_ref, vb_ref, o_ref, s0_ref, s1_ref, *,
                 n_blocks, tk, valid_last):
    tq = q_ref.shape[2]
    heads = ((q_ref[0, 0], kta_ref, va_ref), (q_ref[0, 1], ktb_ref, vb_ref))

    def scores(j, s_ref):
        for h, (q, kt_ref, _) in enumerate(heads):
            s_ref[h] = jnp.dot(q, kt_ref[0, 0, j], preferred_element_type=F32)

    def update(j, s_ref, carry, masked=False):
        out = []
        for h, ((_, _, v_ref), (m, acc)) in enumerate(zip(heads, carry)):
            s = s_ref[h]
            if masked:
                col = lax.broadcasted_iota(jnp.int32, s.shape, 1)
                s = jnp.where(col < valid_last, s, NEG_BIG)
            m_new = jnp.maximum(m, jnp.max(s, axis=1, keepdims=True))
            alpha = jnp.exp2(m - m_new)
            p = jnp.exp2(s - m_new).astype(BF16)
            v = v_ref[0, 0, pl.ds(pl.multiple_of(j * tk, tk), tk), :]
            out.append((m_new, alpha * acc + jnp.dot(p, v, preferred_element_type=F32)))
        return tuple(out)

    def pair(i, carry):
        scores(2 * i + 1, s1_ref)
        carry = update(2 * i, s0_ref, carry)
        scores(2 * i + 2, s0_ref)
        return update(2 * i + 1, s1_ref, carry)

    last = n_blocks - 1
    carry = ((jnp.full((tq, 1), NEG_BIG, F32), jnp.zeros((tq, LANES), F32)),) * 2
    scores(0, s0_ref)
    carry = lax.fori_loop(0, last // 2, pair, carry)
    if last % 2:
        scores(last, s1_ref)
        carry = update(last - 1, s0_ref, carry)
        (_, acc_a), (_, acc_b) = update(last, s1_ref, carry, masked=True)
    else:
        (_, acc_a), (_, acc_b) = update(last, s0_ref, carry, masked=True)
    o_a = acc_a / pltpu.roll(acc_a, HEAD_DIM, 1)
    o_b = acc_b / pltpu.roll(acc_b, HEAD_DIM, 1)
    lane = lax.broadcasted_iota(jnp.int32, o_a.shape, 1)
    o_ref[0] = jnp.where(lane < HEAD_DIM, o_a, o_b).astype(BF16)


def _k_slab(h):
    return jnp.where(h < GQA_HEADS, 0, h - (GQA_HEADS - 1))


def _v_slab(h):
    gqa = 2 * (h // (GQA_HEADS // GQA_KV_HEADS)) + h % 2
    return jnp.where(h < GQA_HEADS, gqa, h - GQA_HEADS + 2 * GQA_KV_HEADS)


def _attention(q, kt, v, seq_len, tq):
    nb, _, lp, _ = q.shape
    nt, tk = kt.shape[2], kt.shape[4]
    valid_last = seq_len - (nt - 1) * tk
    assert 0 < valid_last <= tk
    kspec = lambda par: pl.BlockSpec((1, 1, nt, LANES, tk),
                                     lambda b, j, i: (b, _k_slab(2 * j + par), 0, 0, 0))
    vspec = lambda par: pl.BlockSpec((1, 1, lp, LANES),
                                     lambda b, j, i: (b, _v_slab(2 * j + par), 0, 0))
    return pl.pallas_call(
        functools.partial(_attn_kernel, n_blocks=nt, tk=tk, valid_last=valid_last),
        grid=(nb, N_HEAD_SLOTS // 2, lp // tq),
        in_specs=[pl.BlockSpec((1, 2, tq, LANES), lambda b, j, i: (b, j, i, 0)),
                  kspec(0), kspec(1), vspec(0), vspec(1)],
        out_specs=pl.BlockSpec((1, tq, LANES), lambda b, j, i: (b, i, j)),
        out_shape=jax.ShapeDtypeStruct((nb, lp, (N_HEAD_SLOTS // 2) * LANES), BF16),
        scratch_shapes=[pltpu.VMEM((2, tq, tk), F32), pltpu.VMEM((2, tq, tk), F32)],
        compiler_params=_cparams(("parallel", "parallel", "parallel")),
        name="attention",
    )(q, kt, kt, v, v)


def _post_kernel(o_ref, g_ref, h_ref, wog_ref, wom_ref, wout_ref, fn_ref, wrh_ref, wrl_ref,
                 h2_ref, xn_ref, aff_ref, *, seq_len):
    tm = o_ref.shape[1]
    half = o_ref.shape[2] // 2
    o = o_ref[0]
    a = jnp.dot(o[:, :half], wog_ref[...], preferred_element_type=F32)
    b = jnp.dot(o[:, half:], wom_ref[...], preferred_element_type=F32)
    g = g_ref[0].astype(F32)
    mixed = (g[:, :D_MODEL] * a + g[:, D_MODEL:] * b).astype(BF16)
    h2 = h_ref[0] + jnp.dot(mixed, wout_ref[...], preferred_element_type=F32)
    h2_ref[0] = h2
    xn = _rms(h2, fn_ref[...])
    xn_ref[0] = xn.astype(BF16)
    hi = xn.astype(BF16)
    lo = (xn - hi.astype(F32)).astype(BF16)
    logits = (jnp.dot(hi, wrh_ref[...], preferred_element_type=F32)
              + jnp.dot(lo, wrh_ref[...], preferred_element_type=F32)
              + jnp.dot(hi, wrl_ref[...], preferred_element_type=F32))
    lane = lax.broadcasted_iota(jnp.int32, logits.shape, 1)
    logits = jnp.where(lane < N_EXPERTS, logits, NEG_BIG)
    e = jnp.exp(logits - jnp.max(logits, axis=1, keepdims=True))
    aff = e / jnp.sum(e, axis=1, keepdims=True)
    row = pl.program_id(1) * tm + lax.broadcasted_iota(jnp.int32, logits.shape, 0)
    aff = jnp.where(row < seq_len, aff, -1.0)
    aff_ref[...] = aff.T[:N_EXPERTS]


def _post(o, g, h, wog, wom, wout, fn, wrh, wrl, seq_len, tm):
    nb, lp, d = h.shape
    nt = lp // tm
    full = lambda a: _resident(a)
    row = lambda w: pl.BlockSpec((1, tm, w), lambda b, i: (b, i, 0))
    return pl.pallas_call(
        functools.partial(_post_kernel, seq_len=seq_len),
        grid=(nb, nt),
        in_specs=[row(d), row(2 * d), row(d), full(wog), full(wom), full(wout), full(fn),
                  full(wrh), full(wrl)],
        out_specs=[row(d), row(d), pl.BlockSpec((N_EXPERTS, tm), lambda b, i: (0, b * nt + i))],
        out_shape=[jax.ShapeDtypeStruct((nb, lp, d), F32),
                   jax.ShapeDtypeStruct((nb, lp, d), BF16),
                   jax.ShapeDtypeStruct((N_EXPERTS, nb * lp), F32)],
        compiler_params=_cparams(("parallel", "parallel")),
        name="merge_router",
    )(o, g, h, wog, wom, wout, fn, wrh, wrl)


def _ffn_kernel(x_ref, wg_ref, wu_ref, wd_ref, y_ref):
    x = x_ref[0]
    hg = jnp.dot(x, wg_ref[0], preferred_element_type=F32)
    hu = jnp.dot(x, wu_ref[0], preferred_element_type=F32)
    act = (hg * jax.nn.sigmoid(hg) * hu).astype(BF16)
    y_ref[0] = jnp.dot(act, wd_ref[0], preferred_element_type=F32).astype(BF16)


def _expert_ffn(xs, wg, wu, wd, tc):
    ne, cp, d = xs.shape
    f = wg.shape[2]
    return pl.pallas_call(
        _ffn_kernel,
        grid=(ne, cp // tc),
        in_specs=[pl.BlockSpec((1, tc, d), lambda e, c: (e, c, 0)),
                  pl.BlockSpec((1, d, f), lambda e, c: (e, 0, 0)),
                  pl.BlockSpec((1, d, f), lambda e, c: (e, 0, 0)),
                  pl.BlockSpec((1, f, d), lambda e, c: (e, 0, 0))],
        out_specs=pl.BlockSpec((1, tc, d), lambda e, c: (e, c, 0)),
        out_shape=jax.ShapeDtypeStruct((ne, cp, d), BF16),
        compiler_params=_cparams(("parallel", "parallel")),
        name="expert_ffn",
    )(xs, wg, wu, wd)


RT = 256
SLOT_CHUNK = 32
MCHUNK = 256
EGROUP = 4
N_EGROUPS = N_EXPERTS // EGROUP
GROUP_ROWS = -(-(EGROUP * (RT + 2 * (SLOT_CHUNK - 1))) // MCHUNK) * MCHUNK
STAGE_ROWS = N_EGROUPS * GROUP_ROWS


def _select_kernel(aff_ref, ridx_ref, gsel_ref, *, cap, idx_bits):
    aff = aff_ref[...]
    bits = lax.bitcast_convert_type(aff, jnp.int32)
    ridx = ridx_ref[...]

    def count(pred):
        return jnp.sum(jnp.where(pred, 1.0, 0.0), axis=1, keepdims=True)

    def value_bit(i, prefix):
        cand = prefix | jnp.left_shift(jnp.int32(1), 30 - i)
        return jnp.where(count(bits >= cand) >= cap, cand, prefix)

    thr = lax.fori_loop(0, 31, value_bit, jnp.zeros((N_EXPERTS, 1), jnp.int32))
    above = bits > thr
    tied = bits == thr
    need = cap - count(above)

    def index_bit(i, prefix):
        cand = prefix | jnp.left_shift(jnp.int32(1), idx_bits - 1 - i)
        return jnp.where(count(tied & (ridx < cand)) < need, cand, prefix)

    last = lax.fori_loop(0, idx_bits, index_bit, jnp.zeros((N_EXPERTS, 1), jnp.int32))
    gsel_ref[...] = jnp.where(above | (tied & (ridx <= last)), aff, -1.0)


def _select(aff, ridx, cap, idx_bits):
    return pl.pallas_call(
        functools.partial(_select_kernel, cap=cap, idx_bits=idx_bits),
        out_shape=jax.ShapeDtypeStruct(aff.shape, F32),
        compiler_params=pltpu.CompilerParams(vmem_limit_bytes=VMEM_LIMIT),
        name="select",
    )(aff, ridx)


def _rank_kernel(gsel_ref, pos_ref, toff_ref, off_ref):
    @pl.when(pl.program_id(0) == 0)
    def _():
        off_ref[...] = jnp.zeros_like(off_ref)

    picked = gsel_ref[...] >= 0.0
    ones = jnp.where(picked, 1.0, 0.0)
    r = lax.broadcasted_iota(jnp.int32, (RT, RT), 0)
    c = lax.broadcasted_iota(jnp.int32, (RT, RT), 1)
    before = jnp.dot(ones.astype(BF16), (r < c).astype(BF16), preferred_element_type=F32)
    off = off_ref[...]
    toff_ref[0] = off.astype(jnp.int32)
    pos_ref[...] = jnp.where(picked, off[:, :1] + before, -1.0).astype(jnp.int32)
    off_ref[...] = off + jnp.sum(ones, axis=1, keepdims=True)


def _rank(gsel):
    ne, n = gsel.shape
    nt = n // RT
    return pl.pallas_call(
        _rank_kernel,
        grid=(nt,),
        in_specs=[pl.BlockSpec((ne, RT), lambda j: (0, j))],
        out_specs=[pl.BlockSpec((ne, RT), lambda j: (0, j)),
                   pl.BlockSpec((1, ne, LANES), lambda j: (j, 0, 0))],
        out_shape=[jax.ShapeDtypeStruct((ne, n), jnp.int32),
                   jax.ShapeDtypeStruct((nt, ne, LANES), jnp.int32)],
        scratch_shapes=[pltpu.VMEM((ne, LANES), F32)],
        compiler_params=_cparams(("arbitrary",)),
        name="rank",
    )(gsel)


def _tile_layout(toff_ref, j, live=True):
    out = []
    for e in range(N_EXPERTS):
        if e % EGROUP == 0:
            moff = (e // EGROUP) * GROUP_ROWS
        s0 = toff_ref[j * N_EXPERTS + e]
        s1 = toff_ref[(j + 1) * N_EXPERTS + e]
        s0al = (s0 // SLOT_CHUNK) * SLOT_CHUNK
        span = jnp.where(live, s1 - s0al, 0)
        nch = (span + SLOT_CHUNK - 1) // SLOT_CHUNK
        out.append((s0al, span, nch, moff))
        moff = moff + nch * SLOT_CHUNK
    return out


def _slot_chunk(ref, start):
    return ref.at[pl.ds(pl.multiple_of(start, SLOT_CHUNK), SLOT_CHUNK)]


def _group_rows(layout, gi):
    _, _, nch, moff = layout[gi * EGROUP + EGROUP - 1]
    return moff + nch * SLOT_CHUNK - gi * GROUP_ROWS


def _dispatch_kernel(toff_ref, x_ref, pos_ref, xs_ref, stage_ref, carry_ref, zero_ref, sem, cnt_ref,
                     *, cap, cp):
    j = pl.program_id(0)
    nt = pl.num_programs(0)
    slot = j % 2

    def wait_chunks(n, s):
        def body(_, c):
            pltpu.make_async_copy(_slot_chunk(zero_ref, 0), _slot_chunk(xs_ref.at[0], 0), sem.at[s]).wait()
            return c
        lax.fori_loop(0, n, body, 0)

    @pl.when(j == 0)
    def _():
        carry_ref[...] = jnp.zeros_like(carry_ref)
        zero_ref[...] = jnp.zeros_like(zero_ref)
        cnt_ref[0] = 0
        cnt_ref[1] = 0

    wait_chunks(cnt_ref[slot], slot)

    layout = _tile_layout(toff_ref, j)
    pos = pos_ref[...]
    x = x_ref[...]
    stage = stage_ref.at[slot]
    for gi in range(N_EGROUPS):
        base = gi * GROUP_ROWS
        keys = [jnp.where(pos[e:e + 1] >= 0, pos[e:e + 1] + (layout[e][3] - base - layout[e][0]), -1)
                for e in range(gi * EGROUP, (gi + 1) * EGROUP)]

        def compact(mc, c, keys=keys, base=base):
            row = mc * MCHUNK + lax.broadcasted_iota(jnp.int32, (MCHUNK, RT), 0)
            hit = keys[0] == row
            for k in keys[1:]:
                hit = hit | (k == row)
            rows = jnp.dot(jnp.where(hit, 1.0, 0.0).astype(BF16), x, preferred_element_type=F32)
            stage[pl.ds(pl.multiple_of(base + mc * MCHUNK, MCHUNK), MCHUNK), :] = rows.astype(BF16)
            return c
        lax.fori_loop(0, (_group_rows(layout, gi) + MCHUNK - 1) // MCHUNK, compact, 0)

    sent = 0
    for e, (s0al, span, _, moff) in enumerate(layout):
        nfull = span // SLOT_CHUNK
        carry = _slot_chunk(carry_ref, e * SLOT_CHUNK)

        @pl.when(span > 0)
        def _():
            head = _slot_chunk(stage, moff)
            head[...] = head[...] + carry[...]

            def send(k, c):
                pltpu.make_async_copy(_slot_chunk(stage, moff + k * SLOT_CHUNK),
                                      _slot_chunk(xs_ref.at[e], s0al + k * SLOT_CHUNK),
                                      sem.at[slot]).start()
                return c
            lax.fori_loop(0, nfull, send, 0)
            tail = _slot_chunk(stage, moff + nfull * SLOT_CHUNK)
            keep = span - nfull * SLOT_CHUNK > 0
            carry[...] = jnp.where(keep, tail[...], jnp.zeros_like(tail[...]))
        sent = sent + nfull
    cnt_ref[slot] = sent

    @pl.when(j == nt - 1)
    def _():
        first = (cap // SLOT_CHUNK) * SLOT_CHUNK
        n_tail = (cp - first) // SLOT_CHUNK
        for e in range(N_EXPERTS):
            for k in range(n_tail):
                src = _slot_chunk(carry_ref, e * SLOT_CHUNK) if k == 0 else _slot_chunk(zero_ref, 0)
                pltpu.make_async_copy(src, _slot_chunk(xs_ref.at[e], first + k * SLOT_CHUNK),
                                      sem.at[slot]).start()
        wait_chunks(cnt_ref[slot] + N_EXPERTS * n_tail, slot)
        wait_chunks(cnt_ref[1 - slot], 1 - slot)


def _dispatch(toff, xn, pos, tile0, cap, cp):
    ne, n = pos.shape
    d = xn.shape[1]
    grid_spec = pltpu.PrefetchScalarGridSpec(
        num_scalar_prefetch=1,
        grid=(n // RT,),
        in_specs=[pl.BlockSpec((RT, d), lambda j, toff: (tile0 + j, 0)),
                  pl.BlockSpec((ne, RT), lambda j, toff: (0, j))],
        out_specs=pl.BlockSpec(memory_space=pl.ANY),
        scratch_shapes=[pltpu.VMEM((2, STAGE_ROWS, d), BF16),
                        pltpu.VMEM((ne * SLOT_CHUNK, d), BF16),
                        pltpu.VMEM((SLOT_CHUNK, d), BF16),
                        pltpu.SemaphoreType.DMA((2,)),
                        pltpu.SMEM((2,), jnp.int32)])
    return pl.pallas_call(
        functools.partial(_dispatch_kernel, cap=cap, cp=cp),
        grid_spec=grid_spec,
        out_shape=jax.ShapeDtypeStruct((ne, cp, d), BF16),
        compiler_params=_cparams(("arbitrary",)),
        name="dispatch",
    )(toff, xn, pos)


def _combine_kernel(toff_ref, h_ref, pos_ref, gsel_ref, gain_ref, y_ref, o_ref, ycat_ref, sem,
                    *, tiles_per_seq, out_tiles_per_seq):
    g = pl.program_id(0)
    n = pl.num_programs(0)
    has_output = lambda t: t % tiles_per_seq < out_tiles_per_seq

    def fetch(t, slot):
        for e, (s0al, _, nch, moff) in enumerate(_tile_layout(toff_ref, t, has_output(t))):
            def start(k, c, e=e, s0al=s0al, moff=moff):
                pltpu.make_async_copy(_slot_chunk(y_ref.at[e], s0al + k * SLOT_CHUNK),
                                      _slot_chunk(ycat_ref.at[slot], moff + k * SLOT_CHUNK),
                                      sem.at[slot]).start()
                return c
            lax.fori_loop(0, nch, start, 0)

    @pl.when(g == 0)
    def _():
        ycat_ref[...] = jnp.zeros_like(ycat_ref)
        fetch(0, 0)

    @pl.when(g + 1 < n)
    def _():
        fetch(g + 1, (g + 1) % 2)

    @pl.when(has_output(g))
    def _():
        slot = g % 2
        layout = _tile_layout(toff_ref, g)

        def wait(_, c):
            pltpu.make_async_copy(_slot_chunk(y_ref.at[0], 0), _slot_chunk(ycat_ref.at[slot], 0),
                                  sem.at[slot]).wait()
            return c
        lax.fori_loop(0, sum(nch for _, _, nch, _ in layout), wait, 0)

        pos = pos_ref[...]
        gate = gsel_ref[...]
        acc = h_ref[...]
        for gi in range(N_EGROUPS):
            base = gi * GROUP_ROWS
            experts = range(gi * EGROUP, (gi + 1) * EGROUP)
            keys = [jnp.where(pos[e:e + 1] >= 0, pos[e:e + 1] + (layout[e][3] - base - layout[e][0]), -1)
                    for e in experts]

            def expand(mc, acc, keys=keys, base=base, experts=experts):
                row = mc * MCHUNK + lax.broadcasted_iota(jnp.int32, (MCHUNK, RT), 0)
                w_t = jnp.zeros((MCHUNK, RT), F32)
                for k, e in zip(keys, experts):
                    w_t = jnp.where(k == row, gate[e:e + 1], w_t)
                hi = w_t.astype(BF16)
                lo = (w_t - hi.astype(F32)).astype(BF16)
                rows = ycat_ref[slot, pl.ds(pl.multiple_of(base + mc * MCHUNK, MCHUNK), MCHUNK), :]
                contract_rows = (((0,), (0,)), ((), ()))
                return (acc + lax.dot_general(hi, rows, contract_rows, preferred_element_type=F32)
                        + lax.dot_general(lo, rows, contract_rows, preferred_element_type=F32))
            acc = lax.fori_loop(0, (_group_rows(layout, gi) + MCHUNK - 1) // MCHUNK, expand, acc)
        o_ref[...] = _rms(acc, gain_ref[...])


def _combine(toff, h2, pos, gsel, gain, y, tile0, lp, seq):
    ne, n = pos.shape
    d = h2.shape[1]
    tiles_per_seq, out_tiles_per_seq = lp // RT, seq // RT
    nbg = n // lp

    def out_index(g, toff):
        return ((g // tiles_per_seq) * out_tiles_per_seq
                + jnp.minimum(g % tiles_per_seq, out_tiles_per_seq - 1), 0)

    grid_spec = pltpu.PrefetchScalarGridSpec(
        num_scalar_prefetch=1,
        grid=(n // RT,),
        in_specs=[pl.BlockSpec((RT, d), lambda g, toff: (tile0 + g, 0)),
                  pl.BlockSpec((ne, RT), lambda g, toff: (0, g)),
                  pl.BlockSpec((ne, RT), lambda g, toff: (0, g)),
                  pl.BlockSpec((1, d), lambda g, toff: (0, 0)),
                  pl.BlockSpec(memory_space=pl.ANY)],
        out_specs=pl.BlockSpec((RT, d), out_index),
        scratch_shapes=[pltpu.VMEM((2, STAGE_ROWS, d), BF16), pltpu.SemaphoreType.DMA((2,))])
    return pl.pallas_call(
        functools.partial(_combine_kernel, tiles_per_seq=tiles_per_seq,
                          out_tiles_per_seq=out_tiles_per_seq),
        grid_spec=grid_spec,
        out_shape=jax.ShapeDtypeStruct((nbg * seq, d), F32),
        compiler_params=_cparams(("arbitrary",)),
        name="combine",
    )(toff, h2, pos, gsel, gain, y)


def _rope_tables(seq, lp):
    j = jnp.arange(lp)
    grid_tok = j < seq
    meta_tok = (j >= seq) & (j < seq + N_META)
    row = jnp.where(grid_tok, j // GRID_W, jnp.where(meta_tok, -1, 0)).astype(F32)
    col = jnp.where(grid_tok, j % GRID_W, jnp.where(meta_tok, j - seq, 0)).astype(F32)

    def axis_tables(pos, half):
        inv = ROPE_BASE ** (-jnp.arange(half, dtype=F32) / half)
        ang = pos[:, None] * inv[None, :]
        cos, sin = jnp.cos(ang), jnp.sin(ang)
        return jnp.concatenate([cos, cos], 1), jnp.concatenate([-sin, sin], 1)

    def both_axes(half):
        cr, sr = axis_tables(row, half)
        cc, sc = axis_tables(col, half)
        return jnp.concatenate([cr, cc], 1), jnp.concatenate([sr, sc], 1)

    cg, sg = both_axes(HEAD_DIM // 4)
    cg, sg = jnp.tile(cg, (1, 2)), jnp.tile(sg, (1, 2))
    cm32, sm32 = both_axes(MLA_ROPE // 4)
    ones = jnp.ones((lp, MLA_NOPE), F32)
    zeros = jnp.zeros((lp, MLA_NOPE), F32)
    tail1 = jnp.ones((lp, LANES - MLA_NOPE - MLA_ROPE), F32)
    tail0 = jnp.zeros((lp, LANES - MLA_NOPE - MLA_ROPE), F32)
    cm = jnp.concatenate([ones, cm32, tail1], 1)
    sm = jnp.concatenate([zeros, sm32, tail0], 1)
    return cg, sg, cm, sm


def _prep_weights(w_in, w_uq, w_ukv):
    d = w_in.shape[0]
    splits = np.cumsum([GQA_HEADS * HEAD_DIM, GQA_KV_HEADS * HEAD_DIM, GQA_KV_HEADS * HEAD_DIM,
                        MLA_Q_RANK, MLA_KV_RANK, MLA_ROPE])
    wq, wk, wv, wcq, wckv, wkr, wgate = jnp.split(w_in, splits, axis=1)
    wq = wq.reshape(d, GQA_HEADS, HEAD_DIM)
    zq = jnp.zeros_like(wq)
    rep = GQA_HEADS // GQA_KV_HEADS
    in_g0 = (jnp.arange(GQA_HEADS) < rep)[None, :, None]
    wq = jnp.concatenate([jnp.where(in_g0, wq, zq), jnp.where(in_g0, zq, wq)], axis=2)
    wq = wq.reshape(d, GQA_HEADS * LANES)
    wkr = jnp.pad(wkr, ((0, 0), (MLA_NOPE, LANES - MLA_NOPE - MLA_ROPE)))
    w1 = jnp.concatenate([wq, wk, wv, wkr, wcq, wckv], axis=1).astype(BF16)

    wuq = w_uq.reshape(MLA_Q_RANK, MLA_HEADS, MLA_NOPE + MLA_ROPE)
    wuq = jnp.pad(wuq, ((0, 0), (0, 0), (0, LANES - MLA_NOPE - MLA_ROPE)))
    wuq = wuq.reshape(MLA_Q_RANK, MLA_HEADS * LANES).astype(BF16)
    wukv = w_ukv.reshape(MLA_KV_RANK, MLA_HEADS, MLA_NOPE + MLA_V)
    wuk = jnp.pad(wukv[:, :, :MLA_NOPE], ((0, 0), (0, 0), (0, LANES - MLA_NOPE)))
    wuk = wuk.reshape(MLA_KV_RANK, MLA_HEADS * LANES).astype(BF16)
    wuv = wukv[:, :, MLA_NOPE:].reshape(MLA_KV_RANK, MLA_HEADS * MLA_V).astype(BF16)
    return w1, wgate.astype(BF16), wuq, wuk, wuv


def _token_tile(seq):
    for t in (768, 512, 256):
        lp = -(-(seq + N_META) // t) * t
        if lp - t < seq + N_META:
            return t, lp
    raise ValueError(seq)


def _chunk(cp):
    for t in range(2048, 0, -128):
        if cp % t == 0:
            return t
    raise ValueError(cp)


def kernel(x_prompt, x_sample, meta_tokens, attn_norm, w_in, gqa_q_norm, gqa_k_norm, mla_q_norm,
           mla_kv_norm, mla_w_uq, mla_w_ukv, w_o_gqa, w_o_mla, w_out, ffn_norm, w_router, w_gate,
           w_up, w_down, final_norm):
    seq = x_prompt.shape[1]
    assert x_sample.shape[1] == seq and attn_norm.shape[0] == 1
    d = x_prompt.shape[2]
    seq_len = seq + N_META
    tm, lp = _token_tile(seq)
    groups = (x_prompt, x_sample)
    nbs = [g.shape[0] for g in groups]
    nb = sum(nbs)

    meta = meta_tokens.astype(F32)[None]
    h = jnp.concatenate([
        jnp.concatenate([g, jnp.broadcast_to(meta, (g.shape[0], N_META, d)),
                         jnp.zeros((g.shape[0], lp - seq_len, d), F32)], axis=1)
        for g in groups], axis=0)

    w1, wgate, wuq, wuk, wuv = _prep_weights(w_in[0], mla_w_uq[0], mla_w_ukv[0])
    cg, sg, cm, sm = _rope_tables(seq, lp)
    tile2 = lambda v: jnp.tile(v[0].astype(F32), 2)[None]
    q, kt, v, gates = _projection(
        h, attn_norm[0][None], w1, wgate, wuq, wuk, wuv, tile2(gqa_q_norm), tile2(gqa_k_norm),
        mla_q_norm[0][None], mla_kv_norm[0][None], cg, sg, cm, sm, tm)
    o = _attention(q, kt, v, seq_len, tm)

    wr = jnp.pad(w_router[0], ((0, 0), (0, LANES - N_EXPERTS)))
    wrh = wr.astype(BF16)
    wrl = (wr - wrh.astype(F32)).astype(BF16)
    h2, xn, aff_t = _post(o, gates, h, w_o_gqa[0].astype(BF16), w_o_mla[0].astype(BF16),
                          w_out[0].astype(BF16), ffn_norm[0][None], wrh, wrl, seq_len, tm)

    wg_e, wu_e, wd_e = w_gate[0].astype(BF16), w_up[0].astype(BF16), w_down[0].astype(BF16)
    assert seq % RT == 0 and lp % RT == 0
    h2f, xnf = h2.reshape(nb * lp, d), xn.reshape(nb * lp, d)
    outs = []
    b0 = 0
    for nbg in nbs:
        n = nbg * seq_len
        cap = (CAPACITY_FACTOR * n) // N_EXPERTS
        cp = -(-cap // LANES) * LANES
        tile0 = b0 * lp // RT
        j = jnp.arange(nbg * lp, dtype=jnp.int32)
        l = j % lp
        ridx = jnp.where(l < seq_len, (j // lp) * seq_len + jnp.where(l < seq, l + N_META, l - seq), 0)
        gsel = _select(aff_t[:, b0 * lp:(b0 + nbg) * lp], ridx[None], cap, max(n - 1, 1).bit_length())
        pos, tile_off = _rank(gsel)
        toff = jnp.concatenate([tile_off[:, :, 0], jnp.full((1, N_EXPERTS), cap, jnp.int32)]).reshape(-1)
        xs = _dispatch(toff, xnf, pos, tile0, cap, cp)
        y = _expert_ffn(xs, wg_e, wu_e, wd_e, _chunk(cp))
        out = _combine(toff, h2f, pos, gsel, final_norm[None], y, tile0, lp, seq)
        outs.append(out.reshape(nbg, seq, d))
        b0 += nbg
    return tuple(outs)
```

```python
import functools

import jax
import jax.numpy as jnp
import numpy as np
from jax import lax
from jax.experimental import pallas as pl
from jax.experimental.pallas import tpu as pltpu

F32 = jnp.float32
BF16 = jnp.bfloat16

D_MODEL = 1024
N_META = 16
GRID_W = 64
ROPE_BASE = 10000.0
NORM_EPS = 1e-6
GQA_HEADS = 8
GQA_KV_HEADS = 2
HEAD_DIM = 64
MLA_HEADS = 8
MLA_NOPE = 64
MLA_ROPE = 32
MLA_V = 64
MLA_Q_RANK = 384
MLA_KV_RANK = 256
N_EXPERTS = 16
CAPACITY_FACTOR = 2
EXPERT_FF = 512
LANES = 128
N_HEAD_SLOTS = GQA_HEADS + MLA_HEADS
N_K_SLABS = 1 + MLA_HEADS
N_V_SLABS = 2 * GQA_KV_HEADS + MLA_HEADS
LOG2E = float(np.log2(np.e))
VMEM_LIMIT = 56 * 1024 * 1024
NEG_BIG = -1e30
ATTN_UNROLL = 10


def _cparams(sem):
    return pltpu.CompilerParams(dimension_semantics=sem, vmem_limit_bytes=VMEM_LIMIT)


def _resident(a):
    return pl.BlockSpec(a.shape, lambda *_: (0,) * a.ndim, pipeline_mode=pl.Buffered(1))


def _rms(x, gain):
    return x * lax.rsqrt(jnp.mean(x * x, axis=-1, keepdims=True) + NORM_EPS) * gain


def _rope(x, cos, sin_signed, half):
    lane = lax.broadcasted_iota(jnp.int32, x.shape, 1)
    first = (lane % (2 * half)) < half
    partner = jnp.where(first, pltpu.roll(x, LANES - half, 1), pltpu.roll(x, half, 1))
    return x * cos + partner * sin_signed


def _split_dot(x, w_bf16):
    hi = x.astype(BF16)
    lo = (x - hi.astype(F32)).astype(BF16)
    return (jnp.dot(hi, w_bf16, preferred_element_type=F32)
            + jnp.dot(lo, w_bf16, preferred_element_type=F32))


def _proj_kernel(h_ref, an_ref, w1_ref, wg_ref, wuq_ref, wuk_ref, wuv_ref,
                 gq_ref, gk_ref, nq_ref, nkv_ref, cg_ref, sg_ref, cm_ref, sm_ref,
                 q_ref, kt_ref, v_ref, g_ref):
    hn = _rms(h_ref[0], an_ref[...]).astype(BF16)
    p1 = jnp.dot(hn, w1_ref[...], preferred_element_type=F32)
    g_ref[0] = jax.nn.sigmoid(jnp.dot(hn, wg_ref[...], preferred_element_type=F32)).astype(BF16)

    r = lax.broadcasted_iota(jnp.int32, (LANES, LANES), 0) // HEAD_DIM
    c = lax.broadcasted_iota(jnp.int32, (LANES, LANES), 1) // HEAD_DIM
    same_head = (r == c).astype(BF16)
    cg, sg, cm, sm = cg_ref[...], sg_ref[...], cm_ref[...], sm_ref[...]

    def head_norm_rope(x, gain):
        ms = _split_dot(x * x, same_head) * (1.0 / HEAD_DIM)
        return _rope(x * lax.rsqrt(ms + NORM_EPS) * gain, cg, sg, HEAD_DIM // 4)

    lane = lax.broadcasted_iota(jnp.int32, cg.shape, 1)
    low = lane < HEAD_DIM

    def v_slabs(pair):
        swapped = pltpu.roll(pair, HEAD_DIM, 1)
        return [jnp.where(m, x, 1.0).astype(BF16)
                for m, x in ((low, pair), (~low, pair), (~low, swapped), (low, swapped))]

    scale_g = LOG2E / float(np.sqrt(HEAD_DIM))
    for h in range(GQA_HEADS):
        x = p1[:, h * LANES:(h + 1) * LANES]
        q_ref[0, h] = (head_norm_rope(x, gq_ref[...]) * scale_g).astype(BF16)
    o = GQA_HEADS * LANES
    kt_ref[0, 0, 0] = head_norm_rope(p1[:, o:o + LANES], gk_ref[...]).T.astype(BF16)
    v0_lo, v1_hi, v0_hi, v1_lo = v_slabs(p1[:, o + LANES:o + 2 * LANES])
    v_ref[0, 0], v_ref[0, 1], v_ref[0, 2], v_ref[0, 3] = v0_lo, v0_hi, v1_lo, v1_hi
    k_rope = _rope(p1[:, o + 2 * LANES:o + 3 * LANES], cm, sm, MLA_ROPE // 4)
    o += 3 * LANES
    cq = _rms(p1[:, o:o + MLA_Q_RANK], nq_ref[...]).astype(BF16)
    ckv = _rms(p1[:, o + MLA_Q_RANK:o + MLA_Q_RANK + MLA_KV_RANK], nkv_ref[...]).astype(BF16)

    scale_m = LOG2E / float(np.sqrt(MLA_NOPE + MLA_ROPE))
    qm = jnp.dot(cq, wuq_ref[...], preferred_element_type=F32)
    for h in range(MLA_HEADS):
        x = qm[:, h * LANES:(h + 1) * LANES]
        q_ref[0, GQA_HEADS + h] = (_rope(x, cm, sm, MLA_ROPE // 4) * scale_m).astype(BF16)
    km = jnp.dot(ckv, wuk_ref[...], preferred_element_type=F32)
    for h in range(MLA_HEADS):
        kt_ref[0, 1 + h, 0] = (km[:, h * LANES:(h + 1) * LANES] + k_rope).T.astype(BF16)
    vm = jnp.dot(ckv, wuv_ref[...], preferred_element_type=F32)
    for j in range(MLA_HEADS // 2):
        even_lo, odd_hi, _, _ = v_slabs(vm[:, j * LANES:(j + 1) * LANES])
        v_ref[0, 2 * GQA_KV_HEADS + 2 * j] = even_lo
        v_ref[0, 2 * GQA_KV_HEADS + 2 * j + 1] = odd_hi


def _projection(h, an, w1, wg, wuq, wuk, wuv, gq, gk, nq, nkv, cg, sg, cm, sm, tm):
    nb, lp, d = h.shape
    nt = lp // tm
    full = lambda a: _resident(a)
    tab = pl.BlockSpec((tm, LANES), lambda b, i: (i, 0))
    return pl.pallas_call(
        _proj_kernel,
        grid=(nb, nt),
        in_specs=[pl.BlockSpec((1, tm, d), lambda b, i: (b, i, 0)),
                  full(an), full(w1), full(wg), full(wuq), full(wuk), full(wuv),
                  full(gq), full(gk), full(nq), full(nkv), tab, tab, tab, tab],
        out_specs=[pl.BlockSpec((1, N_HEAD_SLOTS, tm, LANES), lambda b, i: (b, 0, i, 0)),
                   pl.BlockSpec((1, N_K_SLABS, 1, LANES, tm), lambda b, i: (b, 0, i, 0, 0)),
                   pl.BlockSpec((1, N_V_SLABS, tm, LANES), lambda b, i: (b, 0, i, 0)),
                   pl.BlockSpec((1, tm, 2 * d), lambda b, i: (b, i, 0))],
        out_shape=[jax.ShapeDtypeStruct((nb, N_HEAD_SLOTS, lp, LANES), BF16),
                   jax.ShapeDtypeStruct((nb, N_K_SLABS, nt, LANES, tm), BF16),
                   jax.ShapeDtypeStruct((nb, N_V_SLABS, lp, LANES), BF16),
                   jax.ShapeDtypeStruct((nb, lp, 2 * d), BF16)],
        compiler_params=_cparams(("parallel", "parallel")),
        name="projection",
    )(h, an, w1, wg, wuq, wuk, wuv, gq, gk, nq, nkv, cg, sg, cm, sm)


def _attn_kernel(q_ref, kta_ref, ktb_ref, va_ref, vb_ref, o_ref, s0_ref, s1_ref, *,
                 n_blocks, tk, valid_last):
    tq = q_ref.shape[2]
    heads = ((q_ref[0, 0], kta_ref, va_ref), (q_ref[0, 1], ktb_ref, vb_ref))

    def scores(j, s_ref):
        for h, (q, kt_ref, _) in enumerate(heads):
            s_ref[h] = jnp.dot(q, kt_ref[0, 0, j], preferred_element_type=F32)

    def update(j, s_ref, carry, masked=False):
        out = []
        for h, ((_, _, v_ref), (m, acc)) in enumerate(zip(heads, carry)):
            s = s_ref[h]
            if masked:
                col = lax.broadcasted_iota(jnp.int32, s.shape, 1)
                s = jnp.where(col < valid_last, s, NEG_BIG)
            m_new = jnp.maximum(m, jnp.max(s, axis=1, keepdims=True))
            alpha = jnp.exp2(m - m_new)
            p = jnp.exp2(s - m_new).astype(BF16)
            v = v_ref[0, 0, pl.ds(pl.multiple_of(j * tk, tk), tk), :]
            out.append((m_new, alpha * acc + jnp.dot(p, v, preferred_element_type=F32)))
        return tuple(out)

    bufs = (s0_ref, s1_ref)

    def run(j0, count, carry):
        for u in range(count):
            scores(j0 + u + 1, bufs[(u + 1) % 2])
            carry = update(j0 + u, bufs[u % 2], carry)
        return carry

    last = n_blocks - 1
    trips, rest = divmod(last, ATTN_UNROLL)
    carry = ((jnp.full((tq, 1), NEG_BIG, F32), jnp.zeros((tq, LANES), F32)),) * 2
    scores(0, s0_ref)
    if trips == 1:
        carry = run(0, ATTN_UNROLL, carry)
    elif trips > 1:
        carry = lax.fori_loop(0, trips, lambda i, c: run(i * ATTN_UNROLL, ATTN_UNROLL, c), carry)
    carry = run(trips * ATTN_UNROLL, rest, carry)
    (_, acc_a), (_, acc_b) = update(last, bufs[last % 2], carry, masked=True)
    o_a = acc_a / pltpu.roll(acc_a, HEAD_DIM, 1)
    o_b = acc_b / pltpu.roll(acc_b, HEAD_DIM, 1)
    lane = lax.broadcasted_iota(jnp.int32, o_a.shape, 1)
    o_ref[0] = jnp.where(lane < HEAD_DIM, o_a, o_b).astype(BF16)


def _k_slab(h):
    return jnp.where(h < GQA_HEADS, 0, h - (GQA_HEADS - 1))


def _v_slab(h):
    gqa = 2 * (h // (GQA_HEADS // GQA_KV_HEADS)) + h % 2
    return jnp.where(h < GQA_HEADS, gqa, h - GQA_HEADS + 2 * GQA_KV_HEADS)


def _attention(q, kt, v, seq_len, tq):
    nb, _, lp, _ = q.shape
    nt, tk = kt.shape[2], kt.shape[4]
    valid_last = seq_len - (nt - 1) * tk
    assert 0 < valid_last <= tk
    kspec = lambda par: pl.BlockSpec((1, 1, nt, LANES, tk),
                                     lambda b, j, i: (b, _k_slab(2 * j + par), 0, 0, 0))
    vspec = lambda par: pl.BlockSpec((1, 1, lp, LANES),
                                     lambda b, j, i: (b, _v_slab(2 * j + par), 0, 0))
    return pl.pallas_call(
        functools.partial(_attn_kernel, n_blocks=nt, tk=tk, valid_last=valid_last),
        grid=(nb, N_HEAD_SLOTS // 2, lp // tq),
        in_specs=[pl.BlockSpec((1, 2, tq, LANES), lambda b, j, i: (b, j, i, 0)),
                  kspec(0), kspec(1), vspec(0), vspec(1)],
        out_specs=pl.BlockSpec((1, tq, LANES), lambda b, j, i: (b, i, j)),
        out_shape=jax.ShapeDtypeStruct((nb, lp, (N_HEAD_SLOTS // 2) * LANES), BF16),
        scratch_shapes=[pltpu.VMEM((2, tq, tk), F32), pltpu.VMEM((2, tq, tk), F32)],
        compiler_params=_cparams(("parallel", "parallel", "parallel")),
        name="attention",
    )(q, kt, kt, v, v)


def _post_kernel(o_ref, g_ref, h_ref, wog_ref, wom_ref, wout_ref, fn_ref, wrh_ref, wrl_ref,
                 h2_ref, xn_ref, aff_ref, *, seq_len):
    tm = o_ref.shape[1]
    half = o_ref.shape[2] // 2
    o = o_ref[0]
    a = jnp.dot(o[:, :half], wog_ref[...], preferred_element_type=F32)
    b = jnp.dot(o[:, half:], wom_ref[...], preferred_element_type=F32)
    g = g_ref[0].astype(F32)
    mixed = (g[:, :D_MODEL] * a + g[:, D_MODEL:] * b).astype(BF16)
    h2 = h_ref[0] + jnp.dot(mixed, wout_ref[...], preferred_element_type=F32)
    h2_ref[0] = h2
    xn = _rms(h2, fn_ref[...])
    xn_ref[0] = xn.astype(BF16)
    hi = xn.astype(BF16)
    lo = (xn - hi.astype(F32)).astype(BF16)
    logits = (jnp.dot(hi, wrh_ref[...], preferred_element_type=F32)
              + jnp.dot(lo, wrh_ref[...], preferred_element_type=F32)
              + jnp.dot(hi, wrl_ref[...], preferred_element_type=F32))
    lane = lax.broadcasted_iota(jnp.int32, logits.shape, 1)
    logits = jnp.where(lane < N_EXPERTS, logits, NEG_BIG)
    e = jnp.exp(logits - jnp.max(logits, axis=1, keepdims=True))
    aff = e / jnp.sum(e, axis=1, keepdims=True)
    row = pl.program_id(1) * tm + lax.broadcasted_iota(jnp.int32, logits.shape, 0)
    aff = jnp.where(row < seq_len, aff, -1.0)
    aff_ref[...] = aff.T[:N_EXPERTS]


def _post(o, g, h, wog, wom, wout, fn, wrh, wrl, seq_len, tm):
    nb, lp, d = h.shape
    nt = lp // tm
    full = lambda a: _resident(a)
    row = lambda w: pl.BlockSpec((1, tm, w), lambda b, i: (b, i, 0))
    return pl.pallas_call(
        functools.partial(_post_kernel, seq_len=seq_len),
        grid=(nb, nt),
        in_specs=[row(d), row(2 * d), row(d), full(wog), full(wom), full(wout), full(fn),
                  full(wrh), full(wrl)],
        out_specs=[row(d), row(d), pl.BlockSpec((N_EXPERTS, tm), lambda b, i: (0, b * nt + i))],
        out_shape=[jax.ShapeDtypeStruct((nb, lp, d), F32),
                   jax.ShapeDtypeStruct((nb, lp, d), BF16),
                   jax.ShapeDtypeStruct((N_EXPERTS, nb * lp), F32)],
        compiler_params=_cparams(("parallel", "parallel")),
        name="merge_router",
    )(o, g, h, wog, wom, wout, fn, wrh, wrl)


def _ffn_kernel(x_ref, wg_ref, wu_ref, wd_ref, y_ref):
    x = x_ref[0]
    hg = jnp.dot(x, wg_ref[0], preferred_element_type=F32)
    hu = jnp.dot(x, wu_ref[0], preferred_element_type=F32)
    act = (hg * jax.nn.sigmoid(hg) * hu).astype(BF16)
    y_ref[0] = jnp.dot(act, wd_ref[0], preferred_element_type=F32).astype(BF16)


def _expert_ffn(xs, wg, wu, wd, tc):
    ne, cp, d = xs.shape
    f = wg.shape[2]
    return pl.pallas_call(
        _ffn_kernel,
        grid=(ne, cp // tc),
        in_specs=[pl.BlockSpec((1, tc, d), lambda e, c: (e, c, 0)),
                  pl.BlockSpec((1, d, f), lambda e, c: (e, 0, 0)),
                  pl.BlockSpec((1, d, f), lambda e, c: (e, 0, 0)),
                  pl.BlockSpec((1, f, d), lambda e, c: (e, 0, 0))],
        out_specs=pl.BlockSpec((1, tc, d), lambda e, c: (e, c, 0)),
        out_shape=jax.ShapeDtypeStruct((ne, cp, d), BF16),
        compiler_params=_cparams(("parallel", "parallel")),
        name="expert_ffn",
    )(xs, wg, wu, wd)


RT = 256
SLOT_CHUNK = 32
MCHUNK = 256
EGROUP = 4
N_EGROUPS = N_EXPERTS // EGROUP
GROUP_ROWS = -(-(EGROUP * (RT + 2 * (SLOT_CHUNK - 1))) // MCHUNK) * MCHUNK
STAGE_ROWS = N_EGROUPS * GROUP_ROWS


def _select_kernel(aff_ref, ridx_ref, gsel_ref, *, cap, idx_bits):
    aff = aff_ref[...]
    bits = lax.bitcast_convert_type(aff, jnp.int32)
    ridx = ridx_ref[...]

    def count(pred):
        return jnp.sum(jnp.where(pred, 1.0, 0.0), axis=1, keepdims=True)

    def value_bit(i, prefix):
        cand = prefix | jnp.left_shift(jnp.int32(1), 30 - i)
        return jnp.where(count(bits >= cand) >= cap, cand, prefix)

    thr = lax.fori_loop(0, 31, value_bit, jnp.zeros((N_EXPERTS, 1), jnp.int32))
    above = bits > thr
    tied = bits == thr
    need = cap - count(above)

    def index_bit(i, prefix):
        cand = prefix | jnp.left_shift(jnp.int32(1), idx_bits - 1 - i)
        return jnp.where(count(tied & (ridx < cand)) < need, cand, prefix)

    last = lax.fori_loop(0, idx_bits, index_bit, jnp.zeros((N_EXPERTS, 1), jnp.int32))
    gsel_ref[...] = jnp.where(above | (tied & (ridx <= last)), aff, -1.0)


def _select(aff, ridx, cap, idx_bits):
    return pl.pallas_call(
        functools.partial(_select_kernel, cap=cap, idx_bits=idx_bits),
        out_shape=jax.ShapeDtypeStruct(aff.shape, F32),
        compiler_params=pltpu.CompilerParams(vmem_limit_bytes=VMEM_LIMIT),
        name="select",
    )(aff, ridx)


def _rank_kernel(gsel_ref, pos_ref, toff_ref, off_ref):
    @pl.when(pl.program_id(0) == 0)
    def _():
        off_ref[...] = jnp.zeros_like(off_ref)

    picked = gsel_ref[...] >= 0.0
    ones = jnp.where(picked, 1.0, 0.0)
    r = lax.broadcasted_iota(jnp.int32, (RT, RT), 0)
    c = lax.broadcasted_iota(jnp.int32, (RT, RT), 1)
    before = jnp.dot(ones.astype(BF16), (r < c).astype(BF16), preferred_element_type=F32)
    off = off_ref[...]
    toff_ref[0] = off.astype(jnp.int32)
    pos_ref[...] = jnp.where(picked, off[:, :1] + before, -1.0).astype(jnp.int32)
    off_ref[...] = off + jnp.sum(ones, axis=1, keepdims=True)


def _rank(gsel):
    ne, n = gsel.shape
    nt = n // RT
    return pl.pallas_call(
        _rank_kernel,
        grid=(nt,),
        in_specs=[pl.BlockSpec((ne, RT), lambda j: (0, j))],
        out_specs=[pl.BlockSpec((ne, RT), lambda j: (0, j)),
                   pl.BlockSpec((1, ne, LANES), lambda j: (j, 0, 0))],
        out_shape=[jax.ShapeDtypeStruct((ne, n), jnp.int32),
                   jax.ShapeDtypeStruct((nt, ne, LANES), jnp.int32)],
        scratch_shapes=[pltpu.VMEM((ne, LANES), F32)],
        compiler_params=_cparams(("arbitrary",)),
        name="rank",
    )(gsel)


def _tile_layout(toff_ref, j, live=True):
    out = []
    for e in range(N_EXPERTS):
        if e % EGROUP == 0:
            moff = (e // EGROUP) * GROUP_ROWS
        s0 = toff_ref[j * N_EXPERTS + e]
        s1 = toff_ref[(j + 1) * N_EXPERTS + e]
        s0al = (s0 // SLOT_CHUNK) * SLOT_CHUNK
        span = jnp.where(live, s1 - s0al, 0)
        nch = (span + SLOT_CHUNK - 1) // SLOT_CHUNK
        out.append((s0al, span, nch, moff))
        moff = moff + nch * SLOT_CHUNK
    return out


def _slot_chunk(ref, start):
    return ref.at[pl.ds(pl.multiple_of(start, SLOT_CHUNK), SLOT_CHUNK)]


def _group_rows(layout, gi):
    _, _, nch, moff = layout[gi * EGROUP + EGROUP - 1]
    return moff + nch * SLOT_CHUNK - gi * GROUP_ROWS


def _dispatch_kernel(toff_ref, x_ref, pos_ref, xs_ref, stage_ref, carry_ref, zero_ref, sem, cnt_ref,
                     *, cap, cp):
    j = pl.program_id(0)
    nt = pl.num_programs(0)
    slot = j % 2

    def wait_chunks(n, s):
        def body(_, c):
            pltpu.make_async_copy(_slot_chunk(zero_ref, 0), _slot_chunk(xs_ref.at[0], 0), sem.at[s]).wait()
            return c
        lax.fori_loop(0, n, body, 0)

    @pl.when(j == 0)
    def _():
        carry_ref[...] = jnp.zeros_like(carry_ref)
        zero_ref[...] = jnp.zeros_like(zero_ref)
        cnt_ref[0] = 0
        cnt_ref[1] = 0

    wait_chunks(cnt_ref[slot], slot)

    layout = _tile_layout(toff_ref, j)
    pos = pos_ref[...]
    x = x_ref[...]
    stage = stage_ref.at[slot]
    for gi in range(N_EGROUPS):
        base = gi * GROUP_ROWS
        keys = [jnp.where(pos[e:e + 1] >= 0, pos[e:e + 1] + (layout[e][3] - base - layout[e][0]), -1)
                for e in range(gi * EGROUP, (gi + 1) * EGROUP)]

        def compact(mc, c, keys=keys, base=base):
            row = mc * MCHUNK + lax.broadcasted_iota(jnp.int32, (MCHUNK, RT), 0)
            hit = keys[0] == row
            for k in keys[1:]:
                hit = hit | (k == row)
            rows = jnp.dot(jnp.where(hit, 1.0, 0.0).astype(BF16), x, preferred_element_type=F32)
            stage[pl.ds(pl.multiple_of(base + mc * MCHUNK, MCHUNK), MCHUNK), :] = rows.astype(BF16)
            return c
        lax.fori_loop(0, (_group_rows(layout, gi) + MCHUNK - 1) // MCHUNK, compact, 0)

    sent = 0
    for e, (s0al, span, _, moff) in enumerate(layout):
        nfull = span // SLOT_CHUNK
        carry = _slot_chunk(carry_ref, e * SLOT_CHUNK)

        @pl.when(span > 0)
        def _():
            head = _slot_chunk(stage, moff)
            head[...] = head[...] + carry[...]

            def send(k, c):
                pltpu.make_async_copy(_slot_chunk(stage, moff + k * SLOT_CHUNK),
                                      _slot_chunk(xs_ref.at[e], s0al + k * SLOT_CHUNK),
                                      sem.at[slot]).start()
                return c
            lax.fori_loop(0, nfull, send, 0)
            tail = _slot_chunk(stage, moff + nfull * SLOT_CHUNK)
            keep = span - nfull * SLOT_CHUNK > 0
            carry[...] = jnp.where(keep, tail[...], jnp.zeros_like(tail[...]))
        sent = sent + nfull
    cnt_ref[slot] = sent

    @pl.when(j == nt - 1)
    def _():
        first = (cap // SLOT_CHUNK) * SLOT_CHUNK
        n_tail = (cp - first) // SLOT_CHUNK
        for e in range(N_EXPERTS):
            for k in range(n_tail):
                src = _slot_chunk(carry_ref, e * SLOT_CHUNK) if k == 0 else _slot_chunk(zero_ref, 0)
                pltpu.make_async_copy(src, _slot_chunk(xs_ref.at[e], first + k * SLOT_CHUNK),
                                      sem.at[slot]).start()
        wait_chunks(cnt_ref[slot] + N_EXPERTS * n_tail, slot)
        wait_chunks(cnt_ref[1 - slot], 1 - slot)


def _dispatch(toff, xn, pos, tile0, cap, cp):
    ne, n = pos.shape
    d = xn.shape[1]
    grid_spec = pltpu.PrefetchScalarGridSpec(
        num_scalar_prefetch=1,
        grid=(n // RT,),
        in_specs=[pl.BlockSpec((RT, d), lambda j, toff: (tile0 + j, 0)),
                  pl.BlockSpec((ne, RT), lambda j, toff: (0, j))],
        out_specs=pl.BlockSpec(memory_space=pl.ANY),
        scratch_shapes=[pltpu.VMEM((2, STAGE_ROWS, d), BF16),
                        pltpu.VMEM((ne * SLOT_CHUNK, d), BF16),
                        pltpu.VMEM((SLOT_CHUNK, d), BF16),
                        pltpu.SemaphoreType.DMA((2,)),
                        pltpu.SMEM((2,), jnp.int32)])
    return pl.pallas_call(
        functools.partial(_dispatch_kernel, cap=cap, cp=cp),
        grid_spec=grid_spec,
        out_shape=jax.ShapeDtypeStruct((ne, cp, d), BF16),
        compiler_params=_cparams(("arbitrary",)),
        name="dispatch",
    )(toff, xn, pos)


def _combine_kernel(toff_ref, h_ref, pos_ref, gsel_ref, gain_ref, y_ref, o_ref, ycat_ref, sem,
                    *, tiles_per_seq, out_tiles_per_seq):
    g = pl.program_id(0)
    n = pl.num_programs(0)
    has_output = lambda t: t % tiles_per_seq < out_tiles_per_seq

    def fetch(t, slot):
        for e, (s0al, _, nch, moff) in enumerate(_tile_layout(toff_ref, t, has_output(t))):
            def start(k, c, e=e, s0al=s0al, moff=moff):
                pltpu.make_async_copy(_slot_chunk(y_ref.at[e], s0al + k * SLOT_CHUNK),
                                      _slot_chunk(ycat_ref.at[slot], moff + k * SLOT_CHUNK),
                                      sem.at[slot]).start()
                return c
            lax.fori_loop(0, nch, start, 0)

    @pl.when(g == 0)
    def _():
        ycat_ref[...] = jnp.zeros_like(ycat_ref)
        fetch(0, 0)

    @pl.when(g + 1 < n)
    def _():
        fetch(g + 1, (g + 1) % 2)

    @pl.when(has_output(g))
    def _():
        slot = g % 2
        layout = _tile_layout(toff_ref, g)

        def wait(_, c):
            pltpu.make_async_copy(_slot_chunk(y_ref.at[0], 0), _slot_chunk(ycat_ref.at[slot], 0),
                                  sem.at[slot]).wait()
            return c
        lax.fori_loop(0, sum(nch for _, _, nch, _ in layout), wait, 0)

        pos = pos_ref[...]
        gate = gsel_ref[...]
        acc = h_ref[...]
        for gi in range(N_EGROUPS):
            base = gi * GROUP_ROWS
            experts = range(gi * EGROUP, (gi + 1) * EGROUP)
            keys = [jnp.where(pos[e:e + 1] >= 0, pos[e:e + 1] + (layout[e][3] - base - layout[e][0]), -1)
                    for e in experts]

            def expand(mc, acc, keys=keys, base=base, experts=experts):
                row = mc * MCHUNK + lax.broadcasted_iota(jnp.int32, (MCHUNK, RT), 0)
                w_t = jnp.zeros((MCHUNK, RT), F32)
                for k, e in zip(keys, experts):
                    w_t = jnp.where(k == row, gate[e:e + 1], w_t)
                hi = w_t.astype(BF16)
                lo = (w_t - hi.astype(F32)).astype(BF16)
                rows = ycat_ref[slot, pl.ds(pl.multiple_of(base + mc * MCHUNK, MCHUNK), MCHUNK), :]
                contract_rows = (((0,), (0,)), ((), ()))
                return (acc + lax.dot_general(hi, rows, contract_rows, preferred_element_type=F32)
                        + lax.dot_general(lo, rows, contract_rows, preferred_element_type=F32))
            acc = lax.fori_loop(0, (_group_rows(layout, gi) + MCHUNK - 1) // MCHUNK, expand, acc)
        o_ref[...] = _rms(acc, gain_ref[...])


def _combine(toff, h2, pos, gsel, gain, y, tile0, lp, seq):
    ne, n = pos.shape
    d = h2.shape[1]
    tiles_per_seq, out_tiles_per_seq = lp // RT, seq // RT
    nbg = n // lp

    def out_index(g, toff):
        return ((g // tiles_per_seq) * out_tiles_per_seq
                + jnp.minimum(g % tiles_per_seq, out_tiles_per_seq - 1), 0)

    grid_spec = pltpu.PrefetchScalarGridSpec(
        num_scalar_prefetch=1,
        grid=(n // RT,),
        in_specs=[pl.BlockSpec((RT, d), lambda g, toff: (tile0 + g, 0)),
                  pl.BlockSpec((ne, RT), lambda g, toff: (0, g)),
                  pl.BlockSpec((ne, RT), lambda g, toff: (0, g)),
                  pl.BlockSpec((1, d), lambda g, toff: (0, 0)),
                  pl.BlockSpec(memory_space=pl.ANY)],
        out_specs=pl.BlockSpec((RT, d), out_index),
        scratch_shapes=[pltpu.VMEM((2, STAGE_ROWS, d), BF16), pltpu.SemaphoreType.DMA((2,))])
    return pl.pallas_call(
        functools.partial(_combine_kernel, tiles_per_seq=tiles_per_seq,
                          out_tiles_per_seq=out_tiles_per_seq),
        grid_spec=grid_spec,
        out_shape=jax.ShapeDtypeStruct((nbg * seq, d), F32),
        compiler_params=_cparams(("arbitrary",)),
        name="combine",
    )(toff, h2, pos, gsel, gain, y)


def _rope_tables(seq, lp):
    j = jnp.arange(lp)
    grid_tok = j < seq
    meta_tok = (j >= seq) & (j < seq + N_META)
    row = jnp.where(grid_tok, j // GRID_W, jnp.where(meta_tok, -1, 0)).astype(F32)
    col = jnp.where(grid_tok, j % GRID_W, jnp.where(meta_tok, j - seq, 0)).astype(F32)

    def axis_tables(pos, half):
        inv = ROPE_BASE ** (-jnp.arange(half, dtype=F32) / half)
        ang = pos[:, None] * inv[None, :]
        cos, sin = jnp.cos(ang), jnp.sin(ang)
        return jnp.concatenate([cos, cos], 1), jnp.concatenate([-sin, sin], 1)

    def both_axes(half):
        cr, sr = axis_tables(row, half)
        cc, sc = axis_tables(col, half)
        return jnp.concatenate([cr, cc], 1), jnp.concatenate([sr, sc], 1)

    cg, sg = both_axes(HEAD_DIM // 4)
    cg, sg = jnp.tile(cg, (1, 2)), jnp.tile(sg, (1, 2))
    cm32, sm32 = both_axes(MLA_ROPE // 4)
    ones = jnp.ones((lp, MLA_NOPE), F32)
    zeros = jnp.zeros((lp, MLA_NOPE), F32)
    tail1 = jnp.ones((lp, LANES - MLA_NOPE - MLA_ROPE), F32)
    tail0 = jnp.zeros((lp, LANES - MLA_NOPE - MLA_ROPE), F32)
    cm = jnp.concatenate([ones, cm32, tail1], 1)
    sm = jnp.concatenate([zeros, sm32, tail0], 1)
    return cg, sg, cm, sm


def _prep_weights(w_in, w_uq, w_ukv):
    d = w_in.shape[0]
    splits = np.cumsum([GQA_HEADS * HEAD_DIM, GQA_KV_HEADS * HEAD_DIM, GQA_KV_HEADS * HEAD_DIM,
                        MLA_Q_RANK, MLA_KV_RANK, MLA_ROPE])
    wq, wk, wv, wcq, wckv, wkr, wgate = jnp.split(w_in, splits, axis=1)
    wq = wq.reshape(d, GQA_HEADS, HEAD_DIM)
    zq = jnp.zeros_like(wq)
    rep = GQA_HEADS // GQA_KV_HEADS
    in_g0 = (jnp.arange(GQA_HEADS) < rep)[None, :, None]
    wq = jnp.concatenate([jnp.where(in_g0, wq, zq), jnp.where(in_g0, zq, wq)], axis=2)
    wq = wq.reshape(d, GQA_HEADS * LANES)
    wkr = jnp.pad(wkr, ((0, 0), (MLA_NOPE, LANES - MLA_NOPE - MLA_ROPE)))
    w1 = jnp.concatenate([wq, wk, wv, wkr, wcq, wckv], axis=1).astype(BF16)

    wuq = w_uq.reshape(MLA_Q_RANK, MLA_HEADS, MLA_NOPE + MLA_ROPE)
    wuq = jnp.pad(wuq, ((0, 0), (0, 0), (0, LANES - MLA_NOPE - MLA_ROPE)))
    wuq = wuq.reshape(MLA_Q_RANK, MLA_HEADS * LANES).astype(BF16)
    wukv = w_ukv.reshape(MLA_KV_RANK, MLA_HEADS, MLA_NOPE + MLA_V)
    wuk = jnp.pad(wukv[:, :, :MLA_NOPE], ((0, 0), (0, 0), (0, LANES - MLA_NOPE)))
    wuk = wuk.reshape(MLA_KV_RANK, MLA_HEADS * LANES).astype(BF16)
    wuv = wukv[:, :, MLA_NOPE:].reshape(MLA_KV_RANK, MLA_HEADS * MLA_V).astype(BF16)
    return w1, wgate.astype(BF16), wuq, wuk, wuv


def _token_tile(seq):
    for t in (768, 512, 256):
        lp = -(-(seq + N_META) // t) * t
        if lp - t < seq + N_META:
            return t, lp
    raise ValueError(seq)


def _chunk(cp):
    for t in range(2048, 0, -128):
        if cp % t == 0:
            return t
    raise ValueError(cp)


def kernel(x_prompt, x_sample, meta_tokens, attn_norm, w_in, gqa_q_norm, gqa_k_norm, mla_q_norm,
           mla_kv_norm, mla_w_uq, mla_w_ukv, w_o_gqa, w_o_mla, w_out, ffn_norm, w_router, w_gate,
           w_up, w_down, final_norm):
    seq = x_prompt.shape[1]
    assert x_sample.shape[1] == seq and attn_norm.shape[0] == 1
    d = x_prompt.shape[2]
    seq_len = seq + N_META
    tm, lp = _token_tile(seq)
    groups = (x_prompt, x_sample)
    nbs = [g.shape[0] for g in groups]
    nb = sum(nbs)

    meta = meta_tokens.astype(F32)[None]
    h = jnp.concatenate([
        jnp.concatenate([g, jnp.broadcast_to(meta, (g.shape[0], N_META, d)),
                         jnp.zeros((g.shape[0], lp - seq_len, d), F32)], axis=1)
        for g in groups], axis=0)

    w1, wgate, wuq, wuk, wuv = _prep_weights(w_in[0], mla_w_uq[0], mla_w_ukv[0])
    cg, sg, cm, sm = _rope_tables(seq, lp)
    tile2 = lambda v: jnp.tile(v[0].astype(F32), 2)[None]
    q, kt, v, gates = _projection(
        h, attn_norm[0][None], w1, wgate, wuq, wuk, wuv, tile2(gqa_q_norm), tile2(gqa_k_norm),
        mla_q_norm[0][None], mla_kv_norm[0][None], cg, sg, cm, sm, tm)
    o = _attention(q, kt, v, seq_len, tm)

    wr = jnp.pad(w_router[0], ((0, 0), (0, LANES - N_EXPERTS)))
    wrh = wr.astype(BF16)
    wrl = (wr - wrh.astype(F32)).astype(BF16)
    h2, xn, aff_t = _post(o, gates, h, w_o_gqa[0].astype(BF16), w_o_mla[0].astype(BF16),
                          w_out[0].astype(BF16), ffn_norm[0][None], wrh, wrl, seq_len, tm)

    wg_e, wu_e, wd_e = w_gate[0].astype(BF16), w_up[0].astype(BF16), w_down[0].astype(BF16)
    assert seq % RT == 0 and lp % RT == 0
    h2f, xnf = h2.reshape(nb * lp, d), xn.reshape(nb * lp, d)
    outs = []
    b0 = 0
    for nbg in nbs:
        n = nbg * seq_len
        cap = (CAPACITY_FACTOR * n) // N_EXPERTS
        cp = -(-cap // LANES) * LANES
        tile0 = b0 * lp // RT
        j = jnp.arange(nbg * lp, dtype=jnp.int32)
        l = j % lp
        ridx = jnp.where(l < seq_len, (j // lp) * seq_len + jnp.where(l < seq, l + N_META, l - seq), 0)
        gsel = _select(aff_t[:, b0 * lp:(b0 + nbg) * lp], ridx[None], cap, max(n - 1, 1).bit_length())
        pos, tile_off = _rank(gsel)
        toff = jnp.concatenate([tile_off[:, :, 0], jnp.full((1, N_EXPERTS), cap, jnp.int32)]).reshape(-1)
        xs = _dispatch(toff, xnf, pos, tile0, cap, cp)
        y = _expert_ffn(xs, wg_e, wu_e, wd_e, _chunk(cp))
        out = _combine(toff, h2f, pos, gsel, final_norm[None], y, tile0, lp, seq)
        outs.append(out.reshape(nbg, seq, d))
        b0 += nbg
    return tuple(outs)
```

```python
import functools

import jax
import jax.numpy as jnp
import numpy as np
from jax import lax
from jax.experimental import pallas as pl
from jax.experimental.pallas import tpu as pltpu

F32 = jnp.float32
BF16 = jnp.bfloat16

D_MODEL = 1024
N_META = 16
GRID_W = 64
ROPE_BASE = 10000.0
NORM_EPS = 1e-6
GQA_HEADS = 8
GQA_KV_HEADS = 2
HEAD_DIM = 64
MLA_HEADS = 8
MLA_NOPE = 64
MLA_ROPE = 32
MLA_V = 64
MLA_Q_RANK = 384
MLA_KV_RANK = 256
N_EXPERTS = 16
CAPACITY_FACTOR = 2
EXPERT_FF = 512
LANES = 128
N_HEAD_SLOTS = GQA_HEADS + MLA_HEADS
N_K_SLABS = 1 + MLA_HEADS
N_V_SLABS = 2 * GQA_KV_HEADS + MLA_HEADS
LOG2E = float(np.log2(np.e))
VMEM_LIMIT = 56 * 1024 * 1024
NEG_BIG = -1e30


def _cparams(sem):
    return pltpu.CompilerParams(dimension_semantics=sem, vmem_limit_bytes=VMEM_LIMIT)


def _resident(a):
    return pl.BlockSpec(a.shape, lambda *_: (0,) * a.ndim, pipeline_mode=pl.Buffered(1))


def _rms(x, gain):
    return x * lax.rsqrt(jnp.mean(x * x, axis=-1, keepdims=True) + NORM_EPS) * gain


def _rope(x, cos, sin_signed, half):
    lane = lax.broadcasted_iota(jnp.int32, x.shape, 1)
    first = (lane % (2 * half)) < half
    partner = jnp.where(first, pltpu.roll(x, LANES - half, 1), pltpu.roll(x, half, 1))
    return x * cos + partner * sin_signed


def _split_dot(x, w_bf16):
    hi = x.astype(BF16)
    lo = (x - hi.astype(F32)).astype(BF16)
    return (jnp.dot(hi, w_bf16, preferred_element_type=F32)
            + jnp.dot(lo, w_bf16, preferred_element_type=F32))


def _proj_kernel(h_ref, an_ref, w1_ref, wg_ref, wuq_ref, wuk_ref, wuv_ref,
                 gq_ref, gk_ref, nq_ref, nkv_ref, cg_ref, sg_ref, cm_ref, sm_ref,
                 q_ref, kt_ref, v_ref, g_ref):
    hn = _rms(h_ref[0], an_ref[...]).astype(BF16)
    p1 = jnp.dot(hn, w1_ref[...], preferred_element_type=F32)
    g_ref[0] = jax.nn.sigmoid(jnp.dot(hn, wg_ref[...], preferred_element_type=F32)).astype(BF16)

    r = lax.broadcasted_iota(jnp.int32, (LANES, LANES), 0) // HEAD_DIM
    c = lax.broadcasted_iota(jnp.int32, (LANES, LANES), 1) // HEAD_DIM
    same_head = (r == c).astype(BF16)
    cg, sg, cm, sm = cg_ref[...], sg_ref[...], cm_ref[...], sm_ref[...]

    def head_norm_rope(x, gain):
        ms = _split_dot(x * x, same_head) * (1.0 / HEAD_DIM)
        return _rope(x * lax.rsqrt(ms + NORM_EPS) * gain, cg, sg, HEAD_DIM // 4)

    lane = lax.broadcasted_iota(jnp.int32, cg.shape, 1)
    low = lane < HEAD_DIM

    def v_slabs(pair):
        swapped = pltpu.roll(pair, HEAD_DIM, 1)
        return [jnp.where(m, x, 1.0).astype(BF16)
                for m, x in ((low, pair), (~low, pair), (~low, swapped), (low, swapped))]

    scale_g = LOG2E / float(np.sqrt(HEAD_DIM))
    for h in range(GQA_HEADS):
        x = p1[:, h * LANES:(h + 1) * LANES]
        q_ref[0, h] = (head_norm_rope(x, gq_ref[...]) * scale_g).astype(BF16)
    o = GQA_HEADS * LANES
    kt_ref[0, 0, 0] = head_norm_rope(p1[:, o:o + LANES], gk_ref[...]).T.astype(BF16)
    v0_lo, v1_hi, v0_hi, v1_lo = v_slabs(p1[:, o + LANES:o + 2 * LANES])
    v_ref[0, 0], v_ref[0, 1], v_ref[0, 2], v_ref[0, 3] = v0_lo, v0_hi, v1_lo, v1_hi
    k_rope = _rope(p1[:, o + 2 * LANES:o + 3 * LANES], cm, sm, MLA_ROPE // 4)
    o += 3 * LANES
    cq = _rms(p1[:, o:o + MLA_Q_RANK], nq_ref[...]).astype(BF16)
    ckv = _rms(p1[:, o + MLA_Q_RANK:o + MLA_Q_RANK + MLA_KV_RANK], nkv_ref[...]).astype(BF16)

    scale_m = LOG2E / float(np.sqrt(MLA_NOPE + MLA_ROPE))
    qm = jnp.dot(cq, wuq_ref[...], preferred_element_type=F32)
    for h in range(MLA_HEADS):
        x = qm[:, h * LANES:(h + 1) * LANES]
        q_ref[0, GQA_HEADS + h] = (_rope(x, cm, sm, MLA_ROPE // 4) * scale_m).astype(BF16)
    km = jnp.dot(ckv, wuk_ref[...], preferred_element_type=F32)
    for h in range(MLA_HEADS):
        kt_ref[0, 1 + h, 0] = (km[:, h * LANES:(h + 1) * LANES] + k_rope).T.astype(BF16)
    vm = jnp.dot(ckv, wuv_ref[...], preferred_element_type=F32)
    for j in range(MLA_HEADS // 2):
        even_lo, odd_hi, _, _ = v_slabs(vm[:, j * LANES:(j + 1) * LANES])
        v_ref[0, 2 * GQA_KV_HEADS + 2 * j] = even_lo
        v_ref[0, 2 * GQA_KV_HEADS + 2 * j + 1] = odd_hi


def _projection(h, an, w1, wg, wuq, wuk, wuv, gq, gk, nq, nkv, cg, sg, cm, sm, tm):
    nb, lp, d = h.shape
    nt = lp // tm
    full = lambda a: _resident(a)
    tab = pl.BlockSpec((tm, LANES), lambda b, i: (i, 0))
    return pl.pallas_call(
        _proj_kernel,
        grid=(nb, nt),
        in_specs=[pl.BlockSpec((1, tm, d), lambda b, i: (b, i, 0)),
                  full(an), full(w1), full(wg), full(wuq), full(wuk), full(wuv),
                  full(gq), full(gk), full(nq), full(nkv), tab, tab, tab, tab],
        out_specs=[pl.BlockSpec((1, N_HEAD_SLOTS, tm, LANES), lambda b, i: (b, 0, i, 0)),
                   pl.BlockSpec((1, N_K_SLABS, 1, LANES, tm), lambda b, i: (b, 0, i, 0, 0)),
                   pl.BlockSpec((1, N_V_SLABS, tm, LANES), lambda b, i: (b, 0, i, 0)),
                   pl.BlockSpec((1, tm, 2 * d), lambda b, i: (b, i, 0))],
        out_shape=[jax.ShapeDtypeStruct((nb, N_HEAD_SLOTS, lp, LANES), BF16),
                   jax.ShapeDtypeStruct((nb, N_K_SLABS, nt, LANES, tm), BF16),
                   jax.ShapeDtypeStruct((nb, N_V_SLABS, lp, LANES), BF16),
                   jax.ShapeDtypeStruct((nb, lp, 2 * d), BF16)],
        compiler_params=_cparams(("parallel", "parallel")),
        name="projection",
    )(h, an, w1, wg, wuq, wuk, wuv, gq, gk, nq, nkv, cg, sg, cm, sm)


def _attn_kernel(q_ref, kta_ref, ktb_ref, va_ref, vb_ref, o_ref, s0_ref, s1_ref, s2_ref, *,
                 tq, n_blocks, tk, valid_last):
    nq = q_ref.shape[2] // tq
    kv = ((kta_ref, va_ref), (ktb_ref, vb_ref))
    last = n_blocks - 1
    buf = lambda j: s2_ref if j == last else (s0_ref, s1_ref)[j % 2]

    def scores(i, j):
        rows = pl.ds(pl.multiple_of(i * tq, tq), tq)
        for h, (kt_ref, _) in enumerate(kv):
            buf(j)[h] = jnp.dot(q_ref[0, h, rows, :], kt_ref[0, 0, j], preferred_element_type=F32)

    def update(j, carry):
        out = []
        for h, ((_, v_ref), (m, acc)) in enumerate(zip(kv, carry)):
            s = buf(j)[h]
            if j == last:
                col = lax.broadcasted_iota(jnp.int32, s.shape, 1)
                s = jnp.where(col < valid_last, s, NEG_BIG)
            m_new = jnp.maximum(m, jnp.max(s, axis=1, keepdims=True))
            alpha = jnp.exp2(m - m_new)
            p = jnp.exp2(s - m_new).astype(BF16)
            out.append((m_new, alpha * acc + jnp.dot(p, v_ref[0, 0, j * tk:(j + 1) * tk, :],
                                                     preferred_element_type=F32)))
        return tuple(out)

    def query_tile(i, c):
        carry = ((jnp.full((tq, 1), NEG_BIG, F32), jnp.zeros((tq, LANES), F32)),) * 2
        for j in range(last):
            scores(i, j + 1)
            carry = update(j, carry)
        nxt = jnp.minimum(i + 1, nq - 1)
        if last:
            scores(nxt, 0)
            (_, acc_a), (_, acc_b) = update(last, carry)
        else:
            (_, acc_a), (_, acc_b) = update(last, carry)
            scores(nxt, 0)
        o_a = acc_a / pltpu.roll(acc_a, HEAD_DIM, 1)
        o_b = acc_b / pltpu.roll(acc_b, HEAD_DIM, 1)
        lane = lax.broadcasted_iota(jnp.int32, o_a.shape, 1)
        o_ref[0, pl.ds(pl.multiple_of(i * tq, tq), tq), :] = jnp.where(lane < HEAD_DIM, o_a, o_b).astype(BF16)
        return c

    scores(0, 0)
    lax.fori_loop(0, nq, query_tile, 0)


def _k_slab(h):
    return jnp.where(h < GQA_HEADS, 0, h - (GQA_HEADS - 1))


def _v_slab(h):
    gqa = 2 * (h // (GQA_HEADS // GQA_KV_HEADS)) + h % 2
    return jnp.where(h < GQA_HEADS, gqa, h - GQA_HEADS + 2 * GQA_KV_HEADS)


def _attention(q, kt, v, seq_len, tq):
    nb, _, lp, _ = q.shape
    nt, tk = kt.shape[2], kt.shape[4]
    valid_last = seq_len - (nt - 1) * tk
    assert 0 < valid_last <= tk
    kspec = lambda par: pl.BlockSpec((1, 1, nt, LANES, tk),
                                     lambda b, j: (b, _k_slab(2 * j + par), 0, 0, 0))
    vspec = lambda par: pl.BlockSpec((1, 1, lp, LANES),
                                     lambda b, j: (b, _v_slab(2 * j + par), 0, 0))
    return pl.pallas_call(
        functools.partial(_attn_kernel, tq=tq, n_blocks=nt, tk=tk, valid_last=valid_last),
        grid=(nb, N_HEAD_SLOTS // 2),
        in_specs=[pl.BlockSpec((1, 2, lp, LANES), lambda b, j: (b, j, 0, 0)),
                  kspec(0), kspec(1), vspec(0), vspec(1)],
        out_specs=pl.BlockSpec((1, lp, LANES), lambda b, j: (b, 0, j)),
        out_shape=jax.ShapeDtypeStruct((nb, lp, (N_HEAD_SLOTS // 2) * LANES), BF16),
        scratch_shapes=[pltpu.VMEM((2, tq, tk), F32)] * 3,
        compiler_params=_cparams(("parallel", "parallel")),
        name="attention",
    )(q, kt, kt, v, v)


def _post_kernel(o_ref, g_ref, h_ref, wog_ref, wom_ref, wout_ref, fn_ref, wrh_ref, wrl_ref,
                 h2_ref, xn_ref, aff_ref, *, seq_len):
    tm = o_ref.shape[1]
    half = o_ref.shape[2] // 2
    o = o_ref[0]
    a = jnp.dot(o[:, :half], wog_ref[...], preferred_element_type=F32)
    b = jnp.dot(o[:, half:], wom_ref[...], preferred_element_type=F32)
    g = g_ref[0].astype(F32)
    mixed = (g[:, :D_MODEL] * a + g[:, D_MODEL:] * b).astype(BF16)
    h2 = h_ref[0] + jnp.dot(mixed, wout_ref[...], preferred_element_type=F32)
    h2_ref[0] = h2
    xn = _rms(h2, fn_ref[...])
    xn_ref[0] = xn.astype(BF16)
    hi = xn.astype(BF16)
    lo = (xn - hi.astype(F32)).astype(BF16)
    logits = (jnp.dot(hi, wrh_ref[...], preferred_element_type=F32)
              + jnp.dot(lo, wrh_ref[...], preferred_element_type=F32)
              + jnp.dot(hi, wrl_ref[...], preferred_element_type=F32))
    lane = lax.broadcasted_iota(jnp.int32, logits.shape, 1)
    logits = jnp.where(lane < N_EXPERTS, logits, NEG_BIG)
    e = jnp.exp(logits - jnp.max(logits, axis=1, keepdims=True))
    aff = e / jnp.sum(e, axis=1, keepdims=True)
    row = pl.program_id(1) * tm + lax.broadcasted_iota(jnp.int32, logits.shape, 0)
    aff = jnp.where(row < seq_len, aff, -1.0)
    aff_ref[...] = aff.T[:N_EXPERTS]


def _post(o, g, h, wog, wom, wout, fn, wrh, wrl, seq_len, tm):
    nb, lp, d = h.shape
    nt = lp // tm
    full = lambda a: _resident(a)
    row = lambda w: pl.BlockSpec((1, tm, w), lambda b, i: (b, i, 0))
    return pl.pallas_call(
        functools.partial(_post_kernel, seq_len=seq_len),
        grid=(nb, nt),
        in_specs=[row(d), row(2 * d), row(d), full(wog), full(wom), full(wout), full(fn),
                  full(wrh), full(wrl)],
        out_specs=[row(d), row(d), pl.BlockSpec((N_EXPERTS, tm), lambda b, i: (0, b * nt + i))],
        out_shape=[jax.ShapeDtypeStruct((nb, lp, d), F32),
                   jax.ShapeDtypeStruct((nb, lp, d), BF16),
                   jax.ShapeDtypeStruct((N_EXPERTS, nb * lp), F32)],
        compiler_params=_cparams(("parallel", "parallel")),
        name="merge_router",
    )(o, g, h, wog, wom, wout, fn, wrh, wrl)


def _ffn_kernel(x_ref, wg_ref, wu_ref, wd_ref, y_ref):
    x = x_ref[0]
    hg = jnp.dot(x, wg_ref[0], preferred_element_type=F32)
    hu = jnp.dot(x, wu_ref[0], preferred_element_type=F32)
    act = (hg * jax.nn.sigmoid(hg) * hu).astype(BF16)
    y_ref[0] = jnp.dot(act, wd_ref[0], preferred_element_type=F32).astype(BF16)


def _expert_ffn(xs, wg, wu, wd, tc):
    ne, cp, d = xs.shape
    f = wg.shape[2]
    return pl.pallas_call(
        _ffn_kernel,
        grid=(ne, cp // tc),
        in_specs=[pl.BlockSpec((1, tc, d), lambda e, c: (e, c, 0)),
                  pl.BlockSpec((1, d, f), lambda e, c: (e, 0, 0)),
                  pl.BlockSpec((1, d, f), lambda e, c: (e, 0, 0)),
                  pl.BlockSpec((1, f, d), lambda e, c: (e, 0, 0))],
        out_specs=pl.BlockSpec((1, tc, d), lambda e, c: (e, c, 0)),
        out_shape=jax.ShapeDtypeStruct((ne, cp, d), BF16),
        compiler_params=_cparams(("parallel", "parallel")),
        name="expert_ffn",
    )(xs, wg, wu, wd)


RT = 256
SLOT_CHUNK = 32
MCHUNK = 256
EGROUP = 4
N_EGROUPS = N_EXPERTS // EGROUP
GROUP_ROWS = -(-(EGROUP * (RT + 2 * (SLOT_CHUNK - 1))) // MCHUNK) * MCHUNK
STAGE_ROWS = N_EGROUPS * GROUP_ROWS


def _select_kernel(aff_ref, ridx_ref, gsel_ref, *, cap, idx_bits):
    aff = aff_ref[...]
    bits = lax.bitcast_convert_type(aff, jnp.int32)
    ridx = ridx_ref[...]

    def count(pred):
        return jnp.sum(jnp.where(pred, 1.0, 0.0), axis=1, keepdims=True)

    def value_bit(i, prefix):
        cand = prefix | jnp.left_shift(jnp.int32(1), 30 - i)
        return jnp.where(count(bits >= cand) >= cap, cand, prefix)

    thr = lax.fori_loop(0, 31, value_bit, jnp.zeros((N_EXPERTS, 1), jnp.int32))
    above = bits > thr
    tied = bits == thr
    need = cap - count(above)

    def index_bit(i, prefix):
        cand = prefix | jnp.left_shift(jnp.int32(1), idx_bits - 1 - i)
        return jnp.where(count(tied & (ridx < cand)) < need, cand, prefix)

    last = lax.fori_loop(0, idx_bits, index_bit, jnp.zeros((N_EXPERTS, 1), jnp.int32))
    gsel_ref[...] = jnp.where(above | (tied & (ridx <= last)), aff, -1.0)


def _select(aff, ridx, cap, idx_bits):
    return pl.pallas_call(
        functools.partial(_select_kernel, cap=cap, idx_bits=idx_bits),
        out_shape=jax.ShapeDtypeStruct(aff.shape, F32),
        compiler_params=pltpu.CompilerParams(vmem_limit_bytes=VMEM_LIMIT),
        name="select",
    )(aff, ridx)


def _rank_kernel(gsel_ref, pos_ref, toff_ref, off_ref):
    @pl.when(pl.program_id(0) == 0)
    def _():
        off_ref[...] = jnp.zeros_like(off_ref)

    picked = gsel_ref[...] >= 0.0
    ones = jnp.where(picked, 1.0, 0.0)
    r = lax.broadcasted_iota(jnp.int32, (RT, RT), 0)
    c = lax.broadcasted_iota(jnp.int32, (RT, RT), 1)
    before = jnp.dot(ones.astype(BF16), (r < c).astype(BF16), preferred_element_type=F32)
    off = off_ref[...]
    toff_ref[0] = off.astype(jnp.int32)
    pos_ref[...] = jnp.where(picked, off[:, :1] + before, -1.0).astype(jnp.int32)
    off_ref[...] = off + jnp.sum(ones, axis=1, keepdims=True)


def _rank(gsel):
    ne, n = gsel.shape
    nt = n // RT
    return pl.pallas_call(
        _rank_kernel,
        grid=(nt,),
        in_specs=[pl.BlockSpec((ne, RT), lambda j: (0, j))],
        out_specs=[pl.BlockSpec((ne, RT), lambda j: (0, j)),
                   pl.BlockSpec((1, ne, LANES), lambda j: (j, 0, 0))],
        out_shape=[jax.ShapeDtypeStruct((ne, n), jnp.int32),
                   jax.ShapeDtypeStruct((nt, ne, LANES), jnp.int32)],
        scratch_shapes=[pltpu.VMEM((ne, LANES), F32)],
        compiler_params=_cparams(("arbitrary",)),
        name="rank",
    )(gsel)


def _tile_layout(toff_ref, j, live=True):
    out = []
    for e in range(N_EXPERTS):
        if e % EGROUP == 0:
            moff = (e // EGROUP) * GROUP_ROWS
        s0 = toff_ref[j * N_EXPERTS + e]
        s1 = toff_ref[(j + 1) * N_EXPERTS + e]
        s0al = (s0 // SLOT_CHUNK) * SLOT_CHUNK
        span = jnp.where(live, s1 - s0al, 0)
        nch = (span + SLOT_CHUNK - 1) // SLOT_CHUNK
        out.append((s0al, span, nch, moff))
        moff = moff + nch * SLOT_CHUNK
    return out


def _slot_chunk(ref, start):
    return ref.at[pl.ds(pl.multiple_of(start, SLOT_CHUNK), SLOT_CHUNK)]


def _group_rows(layout, gi):
    _, _, nch, moff = layout[gi * EGROUP + EGROUP - 1]
    return moff + nch * SLOT_CHUNK - gi * GROUP_ROWS


def _dispatch_kernel(toff_ref, x_ref, pos_ref, xs_ref, stage_ref, carry_ref, zero_ref, sem, cnt_ref,
                     *, cap, cp):
    j = pl.program_id(0)
    nt = pl.num_programs(0)
    slot = j % 2

    def wait_chunks(n, s):
        def body(_, c):
            pltpu.make_async_copy(_slot_chunk(zero_ref, 0), _slot_chunk(xs_ref.at[0], 0), sem.at[s]).wait()
            return c
        lax.fori_loop(0, n, body, 0)

    @pl.when(j == 0)
    def _():
        carry_ref[...] = jnp.zeros_like(carry_ref)
        zero_ref[...] = jnp.zeros_like(zero_ref)
        cnt_ref[0] = 0
        cnt_ref[1] = 0

    wait_chunks(cnt_ref[slot], slot)

    layout = _tile_layout(toff_ref, j)
    pos = pos_ref[...]
    x = x_ref[...]
    stage = stage_ref.at[slot]
    for gi in range(N_EGROUPS):
        base = gi * GROUP_ROWS
        keys = [jnp.where(pos[e:e + 1] >= 0, pos[e:e + 1] + (layout[e][3] - base - layout[e][0]), -1)
                for e in range(gi * EGROUP, (gi + 1) * EGROUP)]

        def compact(mc, c, keys=keys, base=base):
            row = mc * MCHUNK + lax.broadcasted_iota(jnp.int32, (MCHUNK, RT), 0)
            hit = keys[0] == row
            for k in keys[1:]:
                hit = hit | (k == row)
            rows = jnp.dot(jnp.where(hit, 1.0, 0.0).astype(BF16), x, preferred_element_type=F32)
            stage[pl.ds(pl.multiple_of(base + mc * MCHUNK, MCHUNK), MCHUNK), :] = rows.astype(BF16)
            return c
        lax.fori_loop(0, (_group_rows(layout, gi) + MCHUNK - 1) // MCHUNK, compact, 0)

    sent = 0
    for e, (s0al, span, _, moff) in enumerate(layout):
        nfull = span // SLOT_CHUNK
        carry = _slot_chunk(carry_ref, e * SLOT_CHUNK)

        @pl.when(span > 0)
        def _():
            head = _slot_chunk(stage, moff)
            head[...] = head[...] + carry[...]

            def send(k, c):
                pltpu.make_async_copy(_slot_chunk(stage, moff + k * SLOT_CHUNK),
                                      _slot_chunk(xs_ref.at[e], s0al + k * SLOT_CHUNK),
                                      sem.at[slot]).start()
                return c
            lax.fori_loop(0, nfull, send, 0)
            tail = _slot_chunk(stage, moff + nfull * SLOT_CHUNK)
            keep = span - nfull * SLOT_CHUNK > 0
            carry[...] = jnp.where(keep, tail[...], jnp.zeros_like(tail[...]))
        sent = sent + nfull
    cnt_ref[slot] = sent

    @pl.when(j == nt - 1)
    def _():
        first = (cap // SLOT_CHUNK) * SLOT_CHUNK
        n_tail = (cp - first) // SLOT_CHUNK
        for e in range(N_EXPERTS):
            for k in range(n_tail):
                src = _slot_chunk(carry_ref, e * SLOT_CHUNK) if k == 0 else _slot_chunk(zero_ref, 0)
                pltpu.make_async_copy(src, _slot_chunk(xs_ref.at[e], first + k * SLOT_CHUNK),
                                      sem.at[slot]).start()
        wait_chunks(cnt_ref[slot] + N_EXPERTS * n_tail, slot)
        wait_chunks(cnt_ref[1 - slot], 1 - slot)


def _dispatch(toff, xn, pos, tile0, cap, cp):
    ne, n = pos.shape
    d = xn.shape[1]
    grid_spec = pltpu.PrefetchScalarGridSpec(
        num_scalar_prefetch=1,
        grid=(n // RT,),
        in_specs=[pl.BlockSpec((RT, d), lambda j, toff: (tile0 + j, 0)),
                  pl.BlockSpec((ne, RT), lambda j, toff: (0, j))],
        out_specs=pl.BlockSpec(memory_space=pl.ANY),
        scratch_shapes=[pltpu.VMEM((2, STAGE_ROWS, d), BF16),
                        pltpu.VMEM((ne * SLOT_CHUNK, d), BF16),
                        pltpu.VMEM((SLOT_CHUNK, d), BF16),
                        pltpu.SemaphoreType.DMA((2,)),
                        pltpu.SMEM((2,), jnp.int32)])
    return pl.pallas_call(
        functools.partial(_dispatch_kernel, cap=cap, cp=cp),
        grid_spec=grid_spec,
        out_shape=jax.ShapeDtypeStruct((ne, cp, d), BF16),
        compiler_params=_cparams(("arbitrary",)),
        name="dispatch",
    )(toff, xn, pos)


def _combine_kernel(toff_ref, h_ref, pos_ref, gsel_ref, gain_ref, y_ref, o_ref, ycat_ref, sem,
                    *, tiles_per_seq, out_tiles_per_seq):
    g = pl.program_id(0)
    n = pl.num_programs(0)
    has_output = lambda t: t % tiles_per_seq < out_tiles_per_seq

    def fetch(t, slot):
        for e, (s0al, _, nch, moff) in enumerate(_tile_layout(toff_ref, t, has_output(t))):
            def start(k, c, e=e, s0al=s0al, moff=moff):
                pltpu.make_async_copy(_slot_chunk(y_ref.at[e], s0al + k * SLOT_CHUNK),
                                      _slot_chunk(ycat_ref.at[slot], moff + k * SLOT_CHUNK),
                                      sem.at[slot]).start()
                return c
            lax.fori_loop(0, nch, start, 0)

    @pl.when(g == 0)
    def _():
        ycat_ref[...] = jnp.zeros_like(ycat_ref)
        fetch(0, 0)

    @pl.when(g + 1 < n)
    def _():
        fetch(g + 1, (g + 1) % 2)

    @pl.when(has_output(g))
    def _():
        slot = g % 2
        layout = _tile_layout(toff_ref, g)

        def wait(_, c):
            pltpu.make_async_copy(_slot_chunk(y_ref.at[0], 0), _slot_chunk(ycat_ref.at[slot], 0),
                                  sem.at[slot]).wait()
            return c
        lax.fori_loop(0, sum(nch for _, _, nch, _ in layout), wait, 0)

        pos = pos_ref[...]
        gate = gsel_ref[...]
        acc = h_ref[...]
        for gi in range(N_EGROUPS):
            base = gi * GROUP_ROWS
            experts = range(gi * EGROUP, (gi + 1) * EGROUP)
            keys = [jnp.where(pos[e:e + 1] >= 0, pos[e:e + 1] + (layout[e][3] - base - layout[e][0]), -1)
                    for e in experts]

            def expand(mc, acc, keys=keys, base=base, experts=experts):
                row = mc * MCHUNK + lax.broadcasted_iota(jnp.int32, (MCHUNK, RT), 0)
                w_t = jnp.zeros((MCHUNK, RT), F32)
                for k, e in zip(keys, experts):
                    w_t = jnp.where(k == row, gate[e:e + 1], w_t)
                hi = w_t.astype(BF16)
                lo = (w_t - hi.astype(F32)).astype(BF16)
                rows = ycat_ref[slot, pl.ds(pl.multiple_of(base + mc * MCHUNK, MCHUNK), MCHUNK), :]
                contract_rows = (((0,), (0,)), ((), ()))
                return (acc + lax.dot_general(hi, rows, contract_rows, preferred_element_type=F32)
                        + lax.dot_general(lo, rows, contract_rows, preferred_element_type=F32))
            acc = lax.fori_loop(0, (_group_rows(layout, gi) + MCHUNK - 1) // MCHUNK, expand, acc)
        o_ref[...] = _rms(acc, gain_ref[...])


def _combine(toff, h2, pos, gsel, gain, y, tile0, lp, seq):
    ne, n = pos.shape
    d = h2.shape[1]
    tiles_per_seq, out_tiles_per_seq = lp // RT, seq // RT
    nbg = n // lp

    def out_index(g, toff):
        return ((g // tiles_per_seq) * out_tiles_per_seq
                + jnp.minimum(g % tiles_per_seq, out_tiles_per_seq - 1), 0)

    grid_spec = pltpu.PrefetchScalarGridSpec(
        num_scalar_prefetch=1,
        grid=(n // RT,),
        in_specs=[pl.BlockSpec((RT, d), lambda g, toff: (tile0 + g, 0)),
                  pl.BlockSpec((ne, RT), lambda g, toff: (0, g)),
                  pl.BlockSpec((ne, RT), lambda g, toff: (0, g)),
                  pl.BlockSpec((1, d), lambda g, toff: (0, 0)),
                  pl.BlockSpec(memory_space=pl.ANY)],
        out_specs=pl.BlockSpec((RT, d), out_index),
        scratch_shapes=[pltpu.VMEM((2, STAGE_ROWS, d), BF16), pltpu.SemaphoreType.DMA((2,))])
    return pl.pallas_call(
        functools.partial(_combine_kernel, tiles_per_seq=tiles_per_seq,
                          out_tiles_per_seq=out_tiles_per_seq),
        grid_spec=grid_spec,
        out_shape=jax.ShapeDtypeStruct((nbg * seq, d), F32),
        compiler_params=_cparams(("arbitrary",)),
        name="combine",
    )(toff, h2, pos, gsel, gain, y)


def _rope_tables(seq, lp):
    j = jnp.arange(lp)
    grid_tok = j < seq
    meta_tok = (j >= seq) & (j < seq + N_META)
    row = jnp.where(grid_tok, j // GRID_W, jnp.where(meta_tok, -1, 0)).astype(F32)
    col = jnp.where(grid_tok, j % GRID_W, jnp.where(meta_tok, j - seq, 0)).astype(F32)

    def axis_tables(pos, half):
        inv = ROPE_BASE ** (-jnp.arange(half, dtype=F32) / half)
        ang = pos[:, None] * inv[None, :]
        cos, sin = jnp.cos(ang), jnp.sin(ang)
        return jnp.concatenate([cos, cos], 1), jnp.concatenate([-sin, sin], 1)

    def both_axes(half):
        cr, sr = axis_tables(row, half)
        cc, sc = axis_tables(col, half)
        return jnp.concatenate([cr, cc], 1), jnp.concatenate([sr, sc], 1)

    cg, sg = both_axes(HEAD_DIM // 4)
    cg, sg = jnp.tile(cg, (1, 2)), jnp.tile(sg, (1, 2))
    cm32, sm32 = both_axes(MLA_ROPE // 4)
    ones = jnp.ones((lp, MLA_NOPE), F32)
    zeros = jnp.zeros((lp, MLA_NOPE), F32)
    tail1 = jnp.ones((lp, LANES - MLA_NOPE - MLA_ROPE), F32)
    tail0 = jnp.zeros((lp, LANES - MLA_NOPE - MLA_ROPE), F32)
    cm = jnp.concatenate([ones, cm32, tail1], 1)
    sm = jnp.concatenate([zeros, sm32, tail0], 1)
    return cg, sg, cm, sm


def _prep_weights(w_in, w_uq, w_ukv):
    d = w_in.shape[0]
    splits = np.cumsum([GQA_HEADS * HEAD_DIM, GQA_KV_HEADS * HEAD_DIM, GQA_KV_HEADS * HEAD_DIM,
                        MLA_Q_RANK, MLA_KV_RANK, MLA_ROPE])
    wq, wk, wv, wcq, wckv, wkr, wgate = jnp.split(w_in, splits, axis=1)
    wq = wq.reshape(d, GQA_HEADS, HEAD_DIM)
    zq = jnp.zeros_like(wq)
    rep = GQA_HEADS // GQA_KV_HEADS
    in_g0 = (jnp.arange(GQA_HEADS) < rep)[None, :, None]
    wq = jnp.concatenate([jnp.where(in_g0, wq, zq), jnp.where(in_g0, zq, wq)], axis=2)
    wq = wq.reshape(d, GQA_HEADS * LANES)
    wkr = jnp.pad(wkr, ((0, 0), (MLA_NOPE, LANES - MLA_NOPE - MLA_ROPE)))
    w1 = jnp.concatenate([wq, wk, wv, wkr, wcq, wckv], axis=1).astype(BF16)

    wuq = w_uq.reshape(MLA_Q_RANK, MLA_HEADS, MLA_NOPE + MLA_ROPE)
    wuq = jnp.pad(wuq, ((0, 0), (0, 0), (0, LANES - MLA_NOPE - MLA_ROPE)))
    wuq = wuq.reshape(MLA_Q_RANK, MLA_HEADS * LANES).astype(BF16)
    wukv = w_ukv.reshape(MLA_KV_RANK, MLA_HEADS, MLA_NOPE + MLA_V)
    wuk = jnp.pad(wukv[:, :, :MLA_NOPE], ((0, 0), (0, 0), (0, LANES - MLA_NOPE)))
    wuk = wuk.reshape(MLA_KV_RANK, MLA_HEADS * LANES).astype(BF16)
    wuv = wukv[:, :, MLA_NOPE:].reshape(MLA_KV_RANK, MLA_HEADS * MLA_V).astype(BF16)
    return w1, wgate.astype(BF16), wuq, wuk, wuv


def _token_tile(seq):
    for t in (768, 512, 256):
        lp = -(-(seq + N_META) // t) * t
        if lp - t < seq + N_META:
            return t, lp
    raise ValueError(seq)


def _chunk(cp):
    for t in range(2048, 0, -128):
        if cp % t == 0:
            return t
    raise ValueError(cp)


def kernel(x_prompt, x_sample, meta_tokens, attn_norm, w_in, gqa_q_norm, gqa_k_norm, mla_q_norm,
           mla_kv_norm, mla_w_uq, mla_w_ukv, w_o_gqa, w_o_mla, w_out, ffn_norm, w_router, w_gate,
           w_up, w_down, final_norm):
    seq = x_prompt.shape[1]
    assert x_sample.shape[1] == seq and attn_norm.shape[0] == 1
    d = x_prompt.shape[2]
    seq_len = seq + N_META
    tm, lp = _token_tile(seq)
    groups = (x_prompt, x_sample)
    nbs = [g.shape[0] for g in groups]
    nb = sum(nbs)

    meta = meta_tokens.astype(F32)[None]
    h = jnp.concatenate([
        jnp.concatenate([g, jnp.broadcast_to(meta, (g.shape[0], N_META, d)),
                         jnp.zeros((g.shape[0], lp - seq_len, d), F32)], axis=1)
        for g in groups], axis=0)

    w1, wgate, wuq, wuk, wuv = _prep_weights(w_in[0], mla_w_uq[0], mla_w_ukv[0])
    cg, sg, cm, sm = _rope_tables(seq, lp)
    tile2 = lambda v: jnp.tile(v[0].astype(F32), 2)[None]
    q, kt, v, gates = _projection(
        h, attn_norm[0][None], w1, wgate, wuq, wuk, wuv, tile2(gqa_q_norm), tile2(gqa_k_norm),
        mla_q_norm[0][None], mla_kv_norm[0][None], cg, sg, cm, sm, tm)
    o = _attention(q, kt, v, seq_len, tm)

    wr = jnp.pad(w_router[0], ((0, 0), (0, LANES - N_EXPERTS)))
    wrh = wr.astype(BF16)
    wrl = (wr - wrh.astype(F32)).astype(BF16)
    h2, xn, aff_t = _post(o, gates, h, w_o_gqa[0].astype(BF16), w_o_mla[0].astype(BF16),
                          w_out[0].astype(BF16), ffn_norm[0][None], wrh, wrl, seq_len, tm)

    wg_e, wu_e, wd_e = w_gate[0].astype(BF16), w_up[0].astype(BF16), w_down[0].astype(BF16)
    assert seq % RT == 0 and lp % RT == 0
    h2f, xnf = h2.reshape(nb * lp, d), xn.reshape(nb * lp, d)
    outs = []
    b0 = 0
    for nbg in nbs:
        n = nbg * seq_len
        cap = (CAPACITY_FACTOR * n) // N_EXPERTS
        cp = -(-cap // LANES) * LANES
        tile0 = b0 * lp // RT
        j = jnp.arange(nbg * lp, dtype=jnp.int32)
        l = j % lp
        ridx = jnp.where(l < seq_len, (j // lp) * seq_len + jnp.where(l < seq, l + N_META, l - seq), 0)
        gsel = _select(aff_t[:, b0 * lp:(b0 + nbg) * lp], ridx[None], cap, max(n - 1, 1).bit_length())
        pos, tile_off = _rank(gsel)
        toff = jnp.concatenate([tile_off[:, :, 0], jnp.full((1, N_EXPERTS), cap, jnp.int32)]).reshape(-1)
        xs = _dispatch(toff, xnf, pos, tile0, cap, cp)
        y = _expert_ffn(xs, wg_e, wu_e, wd_e, _chunk(cp))
        out = _combine(toff, h2f, pos, gsel, final_norm[None], y, tile0, lp, seq)
        outs.append(out.reshape(nbg, seq, d))
        b0 += nbg
    return tuple(outs)
```

```python
import functools

import jax
import jax.numpy as jnp
import numpy as np
from jax import lax
from jax.experimental import pallas as pl
from jax.experimental.pallas import tpu as pltpu

F32 = jnp.float32
BF16 = jnp.bfloat16

D_MODEL = 1024
N_META = 16
GRID_W = 64
ROPE_BASE = 10000.0
NORM_EPS = 1e-6
GQA_HEADS = 8
GQA_KV_HEADS = 2
HEAD_DIM = 64
MLA_HEADS = 8
MLA_NOPE = 64
MLA_ROPE = 32
MLA_V = 64
MLA_Q_RANK = 384
MLA_KV_RANK = 256
N_EXPERTS = 16
CAPACITY_FACTOR = 2
EXPERT_FF = 512
LANES = 128
N_HEAD_SLOTS = GQA_HEADS + MLA_HEADS
N_K_SLABS = 1 + MLA_HEADS
N_V_SLABS = 2 * GQA_KV_HEADS + MLA_HEADS
LOG2E = float(np.log2(np.e))
VMEM_LIMIT = 56 * 1024 * 1024
NEG_BIG = -1e30


def _cparams(sem):
    return pltpu.CompilerParams(dimension_semantics=sem, vmem_limit_bytes=VMEM_LIMIT)


def _resident(a):
    return pl.BlockSpec(a.shape, lambda *_: (0,) * a.ndim, pipeline_mode=pl.Buffered(1))


def _rms(x, gain):
    return x * lax.rsqrt(jnp.mean(x * x, axis=-1, keepdims=True) + NORM_EPS) * gain


def _rope(x, cos, sin_signed, half):
    lane = lax.broadcasted_iota(jnp.int32, x.shape, 1)
    first = (lane % (2 * half)) < half
    partner = jnp.where(first, pltpu.roll(x, LANES - half, 1), pltpu.roll(x, half, 1))
    return x * cos + partner * sin_signed


def _split_dot(x, w_bf16):
    hi = x.astype(BF16)
    lo = (x - hi.astype(F32)).astype(BF16)
    return (jnp.dot(hi, w_bf16, preferred_element_type=F32)
            + jnp.dot(lo, w_bf16, preferred_element_type=F32))


def _proj_kernel(h_ref, an_ref, w1_ref, wg_ref, wuq_ref, wuk_ref, wuv_ref,
                 gq_ref, gk_ref, nq_ref, nkv_ref, cg_ref, sg_ref, cm_ref, sm_ref,
                 q_ref, kt_ref, v_ref, g_ref):
    hn = _rms(h_ref[0], an_ref[...]).astype(BF16)
    p1 = jnp.dot(hn, w1_ref[...], preferred_element_type=F32)
    g_ref[0] = jax.nn.sigmoid(jnp.dot(hn, wg_ref[...], preferred_element_type=F32)).astype(BF16)

    r = lax.broadcasted_iota(jnp.int32, (LANES, LANES), 0) // HEAD_DIM
    c = lax.broadcasted_iota(jnp.int32, (LANES, LANES), 1) // HEAD_DIM
    same_head = (r == c).astype(BF16)
    cg, sg, cm, sm = cg_ref[...], sg_ref[...], cm_ref[...], sm_ref[...]

    def head_norm_rope(x, gain):
        ms = _split_dot(x * x, same_head) * (1.0 / HEAD_DIM)
        return _rope(x * lax.rsqrt(ms + NORM_EPS) * gain, cg, sg, HEAD_DIM // 4)

    lane = lax.broadcasted_iota(jnp.int32, cg.shape, 1)
    low = lane < HEAD_DIM

    def v_slabs(pair):
        swapped = pltpu.roll(pair, HEAD_DIM, 1)
        return [jnp.where(m, x, 1.0).astype(BF16)
                for m, x in ((low, pair), (~low, pair), (~low, swapped), (low, swapped))]

    scale_g = LOG2E / float(np.sqrt(HEAD_DIM))
    for h in range(GQA_HEADS):
        x = p1[:, h * LANES:(h + 1) * LANES]
        q_ref[0, h] = (head_norm_rope(x, gq_ref[...]) * scale_g).astype(BF16)
    o = GQA_HEADS * LANES
    kt_ref[0, 0, 0] = head_norm_rope(p1[:, o:o + LANES], gk_ref[...]).T.astype(BF16)
    v0_lo, v1_hi, v0_hi, v1_lo = v_slabs(p1[:, o + LANES:o + 2 * LANES])
    v_ref[0, 0], v_ref[0, 1], v_ref[0, 2], v_ref[0, 3] = v0_lo, v0_hi, v1_lo, v1_hi
    k_rope = _rope(p1[:, o + 2 * LANES:o + 3 * LANES], cm, sm, MLA_ROPE // 4)
    o += 3 * LANES
    cq = _rms(p1[:, o:o + MLA_Q_RANK], nq_ref[...]).astype(BF16)
    ckv = _rms(p1[:, o + MLA_Q_RANK:o + MLA_Q_RANK + MLA_KV_RANK], nkv_ref[...]).astype(BF16)

    scale_m = LOG2E / float(np.sqrt(MLA_NOPE + MLA_ROPE))
    qm = jnp.dot(cq, wuq_ref[...], preferred_element_type=F32)
    for h in range(MLA_HEADS):
        x = qm[:, h * LANES:(h + 1) * LANES]
        q_ref[0, GQA_HEADS + h] = (_rope(x, cm, sm, MLA_ROPE // 4) * scale_m).astype(BF16)
    km = jnp.dot(ckv, wuk_ref[...], preferred_element_type=F32)
    for h in range(MLA_HEADS):
        kt_ref[0, 1 + h, 0] = (km[:, h * LANES:(h + 1) * LANES] + k_rope).T.astype(BF16)
    vm = jnp.dot(ckv, wuv_ref[...], preferred_element_type=F32)
    for j in range(MLA_HEADS // 2):
        even_lo, odd_hi, _, _ = v_slabs(vm[:, j * LANES:(j + 1) * LANES])
        v_ref[0, 2 * GQA_KV_HEADS + 2 * j] = even_lo
        v_ref[0, 2 * GQA_KV_HEADS + 2 * j + 1] = odd_hi


def _projection(h, an, w1, wg, wuq, wuk, wuv, gq, gk, nq, nkv, cg, sg, cm, sm, tm):
    nb, lp, d = h.shape
    nt = lp // tm
    full = lambda a: _resident(a)
    tab = pl.BlockSpec((tm, LANES), lambda b, i: (i, 0))
    return pl.pallas_call(
        _proj_kernel,
        grid=(nb, nt),
        in_specs=[pl.BlockSpec((1, tm, d), lambda b, i: (b, i, 0)),
                  full(an), full(w1), full(wg), full(wuq), full(wuk), full(wuv),
                  full(gq), full(gk), full(nq), full(nkv), tab, tab, tab, tab],
        out_specs=[pl.BlockSpec((1, N_HEAD_SLOTS, tm, LANES), lambda b, i: (b, 0, i, 0)),
                   pl.BlockSpec((1, N_K_SLABS, 1, LANES, tm), lambda b, i: (b, 0, i, 0, 0)),
                   pl.BlockSpec((1, N_V_SLABS, tm, LANES), lambda b, i: (b, 0, i, 0)),
                   pl.BlockSpec((1, tm, 2 * d), lambda b, i: (b, i, 0))],
        out_shape=[jax.ShapeDtypeStruct((nb, N_HEAD_SLOTS, lp, LANES), BF16),
                   jax.ShapeDtypeStruct((nb, N_K_SLABS, nt, LANES, tm), BF16),
                   jax.ShapeDtypeStruct((nb, N_V_SLABS, lp, LANES), BF16),
                   jax.ShapeDtypeStruct((nb, lp, 2 * d), BF16)],
        compiler_params=_cparams(("parallel", "parallel")),
        name="projection",
    )(h, an, w1, wg, wuq, wuk, wuv, gq, gk, nq, nkv, cg, sg, cm, sm)


def _attn_kernel(q_ref, kta_ref, ktb_ref, va_ref, vb_ref, o_ref, s0_ref, s1_ref, s2_ref, *,
                 tq, n_blocks, tk, valid_last):
    nq = q_ref.shape[2] // tq
    kv = ((kta_ref, va_ref), (ktb_ref, vb_ref))
    last = n_blocks - 1
    buf = lambda j: s2_ref if j == last else (s0_ref, s1_ref)[j % 2]

    def scores(i, j):
        rows = pl.ds(pl.multiple_of(i * tq, tq), tq)
        for h, (kt_ref, _) in enumerate(kv):
            buf(j)[h] = jnp.dot(q_ref[0, h, rows, :], kt_ref[0, 0, j], preferred_element_type=F32)

    def update(j, carry):
        out = []
        for h, ((_, v_ref), (m, acc)) in enumerate(zip(kv, carry)):
            s = buf(j)[h]
            if j == last:
                col = lax.broadcasted_iota(jnp.int32, s.shape, 1)
                s = jnp.where(col < valid_last, s, NEG_BIG)
            m_new = jnp.maximum(m, jnp.max(s, axis=1, keepdims=True))
            alpha = jnp.exp2(m - m_new)
            p = jnp.exp2(s - m_new).astype(BF16)
            out.append((m_new, alpha * acc + jnp.dot(p, v_ref[0, 0, j * tk:(j + 1) * tk, :],
                                                     preferred_element_type=F32)))
        return tuple(out)

    def query_tile(i, c):
        carry = ((jnp.full((tq, 1), NEG_BIG, F32), jnp.zeros((tq, LANES), F32)),) * 2
        for j in range(last):
            scores(i, j + 1)
            carry = update(j, carry)
        nxt = jnp.minimum(i + 1, nq - 1)
        if last:
            scores(nxt, 0)
            (_, acc_a), (_, acc_b) = update(last, carry)
        else:
            (_, acc_a), (_, acc_b) = update(last, carry)
            scores(nxt, 0)
        o_a = acc_a / pltpu.roll(acc_a, HEAD_DIM, 1)
        o_b = acc_b / pltpu.roll(acc_b, HEAD_DIM, 1)
        lane = lax.broadcasted_iota(jnp.int32, o_a.shape, 1)
        o_ref[0, pl.ds(pl.multiple_of(i * tq, tq), tq), :] = jnp.where(lane < HEAD_DIM, o_a, o_b).astype(BF16)
        return c

    scores(0, 0)
    lax.fori_loop(0, nq, query_tile, 0)


def _k_slab(h):
    return jnp.where(h < GQA_HEADS, 0, h - (GQA_HEADS - 1))


def _v_slab(h):
    gqa = 2 * (h // (GQA_HEADS // GQA_KV_HEADS)) + h % 2
    return jnp.where(h < GQA_HEADS, gqa, h - GQA_HEADS + 2 * GQA_KV_HEADS)


def _attention(q, kt, v, seq_len, tq):
    nb, _, lp, _ = q.shape
    nt, tk = kt.shape[2], kt.shape[4]
    valid_last = seq_len - (nt - 1) * tk
    assert 0 < valid_last <= tk
    kspec = lambda par: pl.BlockSpec((1, 1, nt, LANES, tk),
                                     lambda b, j: (b, _k_slab(2 * j + par), 0, 0, 0))
    vspec = lambda par: pl.BlockSpec((1, 1, lp, LANES),
                                     lambda b, j: (b, _v_slab(2 * j + par), 0, 0))
    return pl.pallas_call(
        functools.partial(_attn_kernel, tq=tq, n_blocks=nt, tk=tk, valid_last=valid_last),
        grid=(nb, N_HEAD_SLOTS // 2),
        in_specs=[pl.BlockSpec((1, 2, lp, LANES), lambda b, j: (b, j, 0, 0)),
                  kspec(0), kspec(1), vspec(0), vspec(1)],
        out_specs=pl.BlockSpec((1, lp, LANES), lambda b, j: (b, 0, j)),
        out_shape=jax.ShapeDtypeStruct((nb, lp, (N_HEAD_SLOTS // 2) * LANES), BF16),
        scratch_shapes=[pltpu.VMEM((2, tq, tk), F32)] * 3,
        compiler_params=_cparams(("parallel", "parallel")),
        name="attention",
    )(q, kt, kt, v, v)


def _post_kernel(o_ref, g_ref, h_ref, wog_ref, wom_ref, wout_ref, fn_ref, wrh_ref, wrl_ref,
                 h2_ref, xn_ref, aff_ref, *, seq_len):
    tm = o_ref.shape[1]
    half = o_ref.shape[2] // 2
    o = o_ref[0]
    a = jnp.dot(o[:, :half], wog_ref[...], preferred_element_type=F32)
    b = jnp.dot(o[:, half:], wom_ref[...], preferred_element_type=F32)
    g = g_ref[0].astype(F32)
    mixed = (g[:, :D_MODEL] * a + g[:, D_MODEL:] * b).astype(BF16)
    h2 = h_ref[0] + jnp.dot(mixed, wout_ref[...], preferred_element_type=F32)
    h2_ref[0] = h2
    xn = _rms(h2, fn_ref[...])
    xn_ref[0] = xn.astype(BF16)
    hi = xn.astype(BF16)
    lo = (xn - hi.astype(F32)).astype(BF16)
    logits = (jnp.dot(hi, wrh_ref[...], preferred_element_type=F32)
              + jnp.dot(lo, wrh_ref[...], preferred_element_type=F32)
              + jnp.dot(hi, wrl_ref[...], preferred_element_type=F32))
    lane = lax.broadcasted_iota(jnp.int32, logits.shape, 1)
    logits = jnp.where(lane < N_EXPERTS, logits, NEG_BIG)
    e = jnp.exp(logits - jnp.max(logits, axis=1, keepdims=True))
    aff = e / jnp.sum(e, axis=1, keepdims=True)
    row = pl.program_id(1) * tm + lax.broadcasted_iota(jnp.int32, logits.shape, 0)
    aff = jnp.where(row < seq_len, aff, -1.0)
    aff_ref[...] = aff.T[:N_EXPERTS]


def _post(o, g, h, wog, wom, wout, fn, wrh, wrl, seq_len, tm):
    nb, lp, d = h.shape
    nt = lp // tm
    full = lambda a: _resident(a)
    row = lambda w: pl.BlockSpec((1, tm, w), lambda b, i: (b, i, 0))
    return pl.pallas_call(
        functools.partial(_post_kernel, seq_len=seq_len),
        grid=(nb, nt),
        in_specs=[row(d), row(2 * d), row(d), full(wog), full(wom), full(wout), full(fn),
                  full(wrh), full(wrl)],
        out_specs=[row(d), row(d), pl.BlockSpec((N_EXPERTS, tm), lambda b, i: (0, b * nt + i))],
        out_shape=[jax.ShapeDtypeStruct((nb, lp, d), F32),
                   jax.ShapeDtypeStruct((nb, lp, d), BF16),
                   jax.ShapeDtypeStruct((N_EXPERTS, nb * lp), F32)],
        compiler_params=_cparams(("parallel", "parallel")),
        name="merge_router",
    )(o, g, h, wog, wom, wout, fn, wrh, wrl)


def _ffn_kernel(x_ref, wg_ref, wu_ref, wd_ref, y_ref):
    x = x_ref[0]
    hg = jnp.dot(x, wg_ref[0], preferred_element_type=F32)
    hu = jnp.dot(x, wu_ref[0], preferred_element_type=F32)
    act = (hg * jax.nn.sigmoid(hg) * hu).astype(BF16)
    y_ref[0] = jnp.dot(act, wd_ref[0], preferred_element_type=F32).astype(BF16)


def _expert_ffn(xs, wg, wu, wd, tc):
    ne, cp, d = xs.shape
    f = wg.shape[2]
    return pl.pallas_call(
        _ffn_kernel,
        grid=(ne, cp // tc),
        in_specs=[pl.BlockSpec((1, tc, d), lambda e, c: (e, c, 0)),
                  pl.BlockSpec((1, d, f), lambda e, c: (e, 0, 0)),
                  pl.BlockSpec((1, d, f), lambda e, c: (e, 0, 0)),
                  pl.BlockSpec((1, f, d), lambda e, c: (e, 0, 0))],
        out_specs=pl.BlockSpec((1, tc, d), lambda e, c: (e, c, 0)),
        out_shape=jax.ShapeDtypeStruct((ne, cp, d), BF16),
        compiler_params=_cparams(("parallel", "parallel")),
        name="expert_ffn",
    )(xs, wg, wu, wd)


RT = 256
SLOT_CHUNK = 16
MCHUNK = 256
EGROUP = 4
N_EGROUPS = N_EXPERTS // EGROUP
GROUP_ROWS = -(-(EGROUP * (RT + 2 * (SLOT_CHUNK - 1))) // MCHUNK) * MCHUNK
STAGE_ROWS = N_EGROUPS * GROUP_ROWS


def _select_kernel(aff_ref, ridx_ref, gsel_ref, *, cap, idx_bits):
    aff = aff_ref[...]
    bits = lax.bitcast_convert_type(aff, jnp.int32)
    ridx = ridx_ref[...]

    def count(pred):
        return jnp.sum(jnp.where(pred, 1.0, 0.0), axis=1, keepdims=True)

    def value_bit(i, prefix):
        cand = prefix | jnp.left_shift(jnp.int32(1), 30 - i)
        return jnp.where(count(bits >= cand) >= cap, cand, prefix)

    thr = lax.fori_loop(0, 31, value_bit, jnp.zeros((N_EXPERTS, 1), jnp.int32))
    above = bits > thr
    tied = bits == thr
    need = cap - count(above)

    def index_bit(i, prefix):
        cand = prefix | jnp.left_shift(jnp.int32(1), idx_bits - 1 - i)
        return jnp.where(count(tied & (ridx < cand)) < need, cand, prefix)

    last = lax.fori_loop(0, idx_bits, index_bit, jnp.zeros((N_EXPERTS, 1), jnp.int32))
    gsel_ref[...] = jnp.where(above | (tied & (ridx <= last)), aff, -1.0)


def _select(aff, ridx, cap, idx_bits):
    return pl.pallas_call(
        functools.partial(_select_kernel, cap=cap, idx_bits=idx_bits),
        out_shape=jax.ShapeDtypeStruct(aff.shape, F32),
        compiler_params=pltpu.CompilerParams(vmem_limit_bytes=VMEM_LIMIT),
        name="select",
    )(aff, ridx)


def _rank_kernel(gsel_ref, pos_ref, toff_ref, off_ref):
    @pl.when(pl.program_id(0) == 0)
    def _():
        off_ref[...] = jnp.zeros_like(off_ref)

    picked = gsel_ref[...] >= 0.0
    ones = jnp.where(picked, 1.0, 0.0)
    r = lax.broadcasted_iota(jnp.int32, (RT, RT), 0)
    c = lax.broadcasted_iota(jnp.int32, (RT, RT), 1)
    before = jnp.dot(ones.astype(BF16), (r < c).astype(BF16), preferred_element_type=F32)
    off = off_ref[...]
    toff_ref[0] = off.astype(jnp.int32)
    pos_ref[...] = jnp.where(picked, off[:, :1] + before, -1.0).astype(jnp.int32)
    off_ref[...] = off + jnp.sum(ones, axis=1, keepdims=True)


def _rank(gsel):
    ne, n = gsel.shape
    nt = n // RT
    return pl.pallas_call(
        _rank_kernel,
        grid=(nt,),
        in_specs=[pl.BlockSpec((ne, RT), lambda j: (0, j))],
        out_specs=[pl.BlockSpec((ne, RT), lambda j: (0, j)),
                   pl.BlockSpec((1, ne, LANES), lambda j: (j, 0, 0))],
        out_shape=[jax.ShapeDtypeStruct((ne, n), jnp.int32),
                   jax.ShapeDtypeStruct((nt, ne, LANES), jnp.int32)],
        scratch_shapes=[pltpu.VMEM((ne, LANES), F32)],
        compiler_params=_cparams(("arbitrary",)),
        name="rank",
    )(gsel)


def _tile_layout(toff_ref, j, live=True):
    out = []
    for e in range(N_EXPERTS):
        if e % EGROUP == 0:
            moff = (e // EGROUP) * GROUP_ROWS
        s0 = toff_ref[j * N_EXPERTS + e]
        s1 = toff_ref[(j + 1) * N_EXPERTS + e]
        s0al = (s0 // SLOT_CHUNK) * SLOT_CHUNK
        span = jnp.where(live, s1 - s0al, 0)
        nch = (span + SLOT_CHUNK - 1) // SLOT_CHUNK
        out.append((s0al, span, nch, moff))
        moff = moff + nch * SLOT_CHUNK
    return out


def _slot_chunk(ref, start):
    return ref.at[pl.ds(pl.multiple_of(start, SLOT_CHUNK), SLOT_CHUNK)]


def _group_rows(layout, gi):
    _, _, nch, moff = layout[gi * EGROUP + EGROUP - 1]
    return moff + nch * SLOT_CHUNK - gi * GROUP_ROWS


def _dispatch_kernel(toff_ref, x_ref, pos_ref, xs_ref, stage_ref, carry_ref, zero_ref, sem, cnt_ref,
                     *, cap, cp):
    j = pl.program_id(0)
    nt = pl.num_programs(0)
    slot = j % 2

    def wait_chunks(n, s):
        def body(_, c):
            pltpu.make_async_copy(_slot_chunk(zero_ref, 0), _slot_chunk(xs_ref.at[0], 0), sem.at[s]).wait()
            return c
        lax.fori_loop(0, n, body, 0)

    @pl.when(j == 0)
    def _():
        carry_ref[...] = jnp.zeros_like(carry_ref)
        zero_ref[...] = jnp.zeros_like(zero_ref)
        cnt_ref[0] = 0
        cnt_ref[1] = 0

    wait_chunks(cnt_ref[slot], slot)

    layout = _tile_layout(toff_ref, j)
    pos = pos_ref[...]
    x = x_ref[...]
    stage = stage_ref.at[slot]
    for gi in range(N_EGROUPS):
        base = gi * GROUP_ROWS
        keys = [jnp.where(pos[e:e + 1] >= 0, pos[e:e + 1] + (layout[e][3] - base - layout[e][0]), -1)
                for e in range(gi * EGROUP, (gi + 1) * EGROUP)]

        def compact(mc, c, keys=keys, base=base):
            row = mc * MCHUNK + lax.broadcasted_iota(jnp.int32, (MCHUNK, RT), 0)
            hit = keys[0] == row
            for k in keys[1:]:
                hit = hit | (k == row)
            rows = jnp.dot(jnp.where(hit, 1.0, 0.0).astype(BF16), x, preferred_element_type=F32)
            stage[pl.ds(pl.multiple_of(base + mc * MCHUNK, MCHUNK), MCHUNK), :] = rows.astype(BF16)
            return c
        lax.fori_loop(0, (_group_rows(layout, gi) + MCHUNK - 1) // MCHUNK, compact, 0)

    sent = 0
    for e, (s0al, span, _, moff) in enumerate(layout):
        nfull = span // SLOT_CHUNK
        carry = _slot_chunk(carry_ref, e * SLOT_CHUNK)

        @pl.when(span > 0)
        def _():
            head = _slot_chunk(stage, moff)
            head[...] = head[...] + carry[...]

            def send(k, c):
                pltpu.make_async_copy(_slot_chunk(stage, moff + k * SLOT_CHUNK),
                                      _slot_chunk(xs_ref.at[e], s0al + k * SLOT_CHUNK),
                                      sem.at[slot]).start()
                return c
            lax.fori_loop(0, nfull, send, 0)
            tail = _slot_chunk(stage, moff + nfull * SLOT_CHUNK)
            keep = span - nfull * SLOT_CHUNK > 0
            carry[...] = jnp.where(keep, tail[...], jnp.zeros_like(tail[...]))
        sent = sent + nfull
    cnt_ref[slot] = sent

    @pl.when(j == nt - 1)
    def _():
        first = (cap // SLOT_CHUNK) * SLOT_CHUNK
        n_tail = (cp - first) // SLOT_CHUNK
        for e in range(N_EXPERTS):
            for k in range(n_tail):
                src = _slot_chunk(carry_ref, e * SLOT_CHUNK) if k == 0 else _slot_chunk(zero_ref, 0)
                pltpu.make_async_copy(src, _slot_chunk(xs_ref.at[e], first + k * SLOT_CHUNK),
                                      sem.at[slot]).start()
        wait_chunks(cnt_ref[slot] + N_EXPERTS * n_tail, slot)
        wait_chunks(cnt_ref[1 - slot], 1 - slot)


def _dispatch(toff, xn, pos, tile0, cap, cp):
    ne, n = pos.shape
    d = xn.shape[1]
    grid_spec = pltpu.PrefetchScalarGridSpec(
        num_scalar_prefetch=1,
        grid=(n // RT,),
        in_specs=[pl.BlockSpec((RT, d), lambda j, toff: (tile0 + j, 0)),
                  pl.BlockSpec((ne, RT), lambda j, toff: (0, j))],
        out_specs=pl.BlockSpec(memory_space=pl.ANY),
        scratch_shapes=[pltpu.VMEM((2, STAGE_ROWS, d), BF16),
                        pltpu.VMEM((ne * SLOT_CHUNK, d), BF16),
                        pltpu.VMEM((SLOT_CHUNK, d), BF16),
                        pltpu.SemaphoreType.DMA((2,)),
                        pltpu.SMEM((2,), jnp.int32)])
    return pl.pallas_call(
        functools.partial(_dispatch_kernel, cap=cap, cp=cp),
        grid_spec=grid_spec,
        out_shape=jax.ShapeDtypeStruct((ne, cp, d), BF16),
        compiler_params=_cparams(("arbitrary",)),
        name="dispatch",
    )(toff, xn, pos)


def _combine_kernel(toff_ref, h_ref, pos_ref, gsel_ref, gain_ref, y_ref, o_ref, ycat_ref, sem,
                    *, tiles_per_seq, out_tiles_per_seq):
    g = pl.program_id(0)
    n = pl.num_programs(0)
    has_output = lambda t: t % tiles_per_seq < out_tiles_per_seq

    def fetch(t, slot):
        for e, (s0al, _, nch, moff) in enumerate(_tile_layout(toff_ref, t, has_output(t))):
            def start(k, c, e=e, s0al=s0al, moff=moff):
                pltpu.make_async_copy(_slot_chunk(y_ref.at[e], s0al + k * SLOT_CHUNK),
                                      _slot_chunk(ycat_ref.at[slot], moff + k * SLOT_CHUNK),
                                      sem.at[slot]).start()
                return c
            lax.fori_loop(0, nch, start, 0)

    @pl.when(g == 0)
    def _():
        ycat_ref[...] = jnp.zeros_like(ycat_ref)
        fetch(0, 0)

    @pl.when(g + 1 < n)
    def _():
        fetch(g + 1, (g + 1) % 2)

    @pl.when(has_output(g))
    def _():
        slot = g % 2
        layout = _tile_layout(toff_ref, g)

        def wait(_, c):
            pltpu.make_async_copy(_slot_chunk(y_ref.at[0], 0), _slot_chunk(ycat_ref.at[slot], 0),
                                  sem.at[slot]).wait()
            return c
        lax.fori_loop(0, sum(nch for _, _, nch, _ in layout), wait, 0)

        pos = pos_ref[...]
        gate = gsel_ref[...]
        acc = h_ref[...]
        for gi in range(N_EGROUPS):
            base = gi * GROUP_ROWS
            experts = range(gi * EGROUP, (gi + 1) * EGROUP)
            keys = [jnp.where(pos[e:e + 1] >= 0, pos[e:e + 1] + (layout[e][3] - base - layout[e][0]), -1)
                    for e in experts]

            def expand(mc, acc, keys=keys, base=base, experts=experts):
                row = mc * MCHUNK + lax.broadcasted_iota(jnp.int32, (MCHUNK, RT), 0)
                w_t = jnp.zeros((MCHUNK, RT), F32)
                for k, e in zip(keys, experts):
                    w_t = jnp.where(k == row, gate[e:e + 1], w_t)
                rows = ycat_ref[slot, pl.ds(pl.multiple_of(base + mc * MCHUNK, MCHUNK), MCHUNK), :]
                contract_rows = (((0,), (0,)), ((), ()))
                return acc + lax.dot_general(w_t.astype(BF16), rows, contract_rows,
                                             preferred_element_type=F32)
            acc = lax.fori_loop(0, (_group_rows(layout, gi) + MCHUNK - 1) // MCHUNK, expand, acc)
        o_ref[...] = _rms(acc, gain_ref[...])


def _combine(toff, h2, pos, gsel, gain, y, tile0, lp, seq):
    ne, n = pos.shape
    d = h2.shape[1]
    tiles_per_seq, out_tiles_per_seq = lp // RT, seq // RT
    nbg = n // lp

    def out_index(g, toff):
        return ((g // tiles_per_seq) * out_tiles_per_seq
                + jnp.minimum(g % tiles_per_seq, out_tiles_per_seq - 1), 0)

    grid_spec = pltpu.PrefetchScalarGridSpec(
        num_scalar_prefetch=1,
        grid=(n // RT,),
        in_specs=[pl.BlockSpec((RT, d), lambda g, toff: (tile0 + g, 0)),
                  pl.BlockSpec((ne, RT), lambda g, toff: (0, g)),
                  pl.BlockSpec((ne, RT), lambda g, toff: (0, g)),
                  pl.BlockSpec((1, d), lambda g, toff: (0, 0)),
                  pl.BlockSpec(memory_space=pl.ANY)],
        out_specs=pl.BlockSpec((RT, d), out_index),
        scratch_shapes=[pltpu.VMEM((2, STAGE_ROWS, d), BF16), pltpu.SemaphoreType.DMA((2,))])
    return pl.pallas_call(
        functools.partial(_combine_kernel, tiles_per_seq=tiles_per_seq,
                          out_tiles_per_seq=out_tiles_per_seq),
        grid_spec=grid_spec,
        out_shape=jax.ShapeDtypeStruct((nbg * seq, d), F32),
        compiler_params=_cparams(("arbitrary",)),
        name="combine",
    )(toff, h2, pos, gsel, gain, y)


def _rope_tables(seq, lp):
    j = jnp.arange(lp)
    grid_tok = j < seq
    meta_tok = (j >= seq) & (j < seq + N_META)
    row = jnp.where(grid_tok, j // GRID_W, jnp.where(meta_tok, -1, 0)).astype(F32)
    col = jnp.where(grid_tok, j % GRID_W, jnp.where(meta_tok, j - seq, 0)).astype(F32)

    def axis_tables(pos, half):
        inv = ROPE_BASE ** (-jnp.arange(half, dtype=F32) / half)
        ang = pos[:, None] * inv[None, :]
        cos, sin = jnp.cos(ang), jnp.sin(ang)
        return jnp.concatenate([cos, cos], 1), jnp.concatenate([-sin, sin], 1)

    def both_axes(half):
        cr, sr = axis_tables(row, half)
        cc, sc = axis_tables(col, half)
        return jnp.concatenate([cr, cc], 1), jnp.concatenate([sr, sc], 1)

    cg, sg = both_axes(HEAD_DIM // 4)
    cg, sg = jnp.tile(cg, (1, 2)), jnp.tile(sg, (1, 2))
    cm32, sm32 = both_axes(MLA_ROPE // 4)
    ones = jnp.ones((lp, MLA_NOPE), F32)
    zeros = jnp.zeros((lp, MLA_NOPE), F32)
    tail1 = jnp.ones((lp, LANES - MLA_NOPE - MLA_ROPE), F32)
    tail0 = jnp.zeros((lp, LANES - MLA_NOPE - MLA_ROPE), F32)
    cm = jnp.concatenate([ones, cm32, tail1], 1)
    sm = jnp.concatenate([zeros, sm32, tail0], 1)
    return cg, sg, cm, sm


def _prep_weights(w_in, w_uq, w_ukv):
    d = w_in.shape[0]
    splits = np.cumsum([GQA_HEADS * HEAD_DIM, GQA_KV_HEADS * HEAD_DIM, GQA_KV_HEADS * HEAD_DIM,
                        MLA_Q_RANK, MLA_KV_RANK, MLA_ROPE])
    wq, wk, wv, wcq, wckv, wkr, wgate = jnp.split(w_in, splits, axis=1)
    wq = wq.reshape(d, GQA_HEADS, HEAD_DIM)
    zq = jnp.zeros_like(wq)
    rep = GQA_HEADS // GQA_KV_HEADS
    in_g0 = (jnp.arange(GQA_HEADS) < rep)[None, :, None]
    wq = jnp.concatenate([jnp.where(in_g0, wq, zq), jnp.where(in_g0, zq, wq)], axis=2)
    wq = wq.reshape(d, GQA_HEADS * LANES)
    wkr = jnp.pad(wkr, ((0, 0), (MLA_NOPE, LANES - MLA_NOPE - MLA_ROPE)))
    w1 = jnp.concatenate([wq, wk, wv, wkr, wcq, wckv], axis=1).astype(BF16)

    wuq = w_uq.reshape(MLA_Q_RANK, MLA_HEADS, MLA_NOPE + MLA_ROPE)
    wuq = jnp.pad(wuq, ((0, 0), (0, 0), (0, LANES - MLA_NOPE - MLA_ROPE)))
    wuq = wuq.reshape(MLA_Q_RANK, MLA_HEADS * LANES).astype(BF16)
    wukv = w_ukv.reshape(MLA_KV_RANK, MLA_HEADS, MLA_NOPE + MLA_V)
    wuk = jnp.pad(wukv[:, :, :MLA_NOPE], ((0, 0), (0, 0), (0, LANES - MLA_NOPE)))
    wuk = wuk.reshape(MLA_KV_RANK, MLA_HEADS * LANES).astype(BF16)
    wuv = wukv[:, :, MLA_NOPE:].reshape(MLA_KV_RANK, MLA_HEADS * MLA_V).astype(BF16)
    return w1, wgate.astype(BF16), wuq, wuk, wuv


def _token_tile(seq):
    for t in (768, 512, 256):
        lp = -(-(seq + N_META) // t) * t
        if lp - t < seq + N_META:
            return t, lp
    raise ValueError(seq)


def _chunk(cp):
    for t in range(2048, 0, -128):
        if cp % t == 0:
            return t
    raise ValueError(cp)


def kernel(x_prompt, x_sample, meta_tokens, attn_norm, w_in, gqa_q_norm, gqa_k_norm, mla_q_norm,
           mla_kv_norm, mla_w_uq, mla_w_ukv, w_o_gqa, w_o_mla, w_out, ffn_norm, w_router, w_gate,
           w_up, w_down, final_norm):
    seq = x_prompt.shape[1]
    assert x_sample.shape[1] == seq and attn_norm.shape[0] == 1
    d = x_prompt.shape[2]
    seq_len = seq + N_META
    tm, lp = _token_tile(seq)
    groups = (x_prompt, x_sample)
    nbs = [g.shape[0] for g in groups]
    nb = sum(nbs)

    meta = meta_tokens.astype(F32)[None]
    h = jnp.concatenate([
        jnp.concatenate([g, jnp.broadcast_to(meta, (g.shape[0], N_META, d)),
                         jnp.zeros((g.shape[0], lp - seq_len, d), F32)], axis=1)
        for g in groups], axis=0)

    w1, wgate, wuq, wuk, wuv = _prep_weights(w_in[0], mla_w_uq[0], mla_w_ukv[0])
    cg, sg, cm, sm = _rope_tables(seq, lp)
    tile2 = lambda v: jnp.tile(v[0].astype(F32), 2)[None]
    q, kt, v, gates = _projection(
        h, attn_norm[0][None], w1, wgate, wuq, wuk, wuv, tile2(gqa_q_norm), tile2(gqa_k_norm),
        mla_q_norm[0][None], mla_kv_norm[0][None], cg, sg, cm, sm, tm)
    o = _attention(q, kt, v, seq_len, tm)

    wr = jnp.pad(w_router[0], ((0, 0), (0, LANES - N_EXPERTS)))
    wrh = wr.astype(BF16)
    wrl = (wr - wrh.astype(F32)).astype(BF16)
    h2, xn, aff_t = _post(o, gates, h, w_o_gqa[0].astype(BF16), w_o_mla[0].astype(BF16),
                          w_out[0].astype(BF16), ffn_norm[0][None], wrh, wrl, seq_len, tm)

    wg_e, wu_e, wd_e = w_gate[0].astype(BF16), w_up[0].astype(BF16), w_down[0].astype(BF16)
    assert seq % RT == 0 and lp % RT == 0
    h2f, xnf = h2.reshape(nb * lp, d), xn.reshape(nb * lp, d)
    outs = []
    b0 = 0
    for nbg in nbs:
        n = nbg * seq_len
        cap = (CAPACITY_FACTOR * n) // N_EXPERTS
        cp = -(-cap // LANES) * LANES
        tile0 = b0 * lp // RT
        j = jnp.arange(nbg * lp, dtype=jnp.int32)
        l = j % lp
        ridx = jnp.where(l < seq_len, (j // lp) * seq_len + jnp.where(l < seq, l + N_META, l - seq), 0)
        gsel = _select(aff_t[:, b0 * lp:(b0 + nbg) * lp], ridx[None], cap, max(n - 1, 1).bit_length())
        pos, tile_off = _rank(gsel)
        toff = jnp.concatenate([tile_off[:, :, 0], jnp.full((1, N_EXPERTS), cap, jnp.int32)]).reshape(-1)
        xs = _dispatch(toff, xnf, pos, tile0, cap, cp)
        y = _expert_ffn(xs, wg_e, wu_e, wd_e, _chunk(cp))
        out = _combine(toff, h2f, pos, gsel, final_norm[None], y, tile0, lp, seq)
        outs.append(out.reshape(nbg, seq, d))
        b0 += nbg
    return tuple(outs)
```
